```python
import math
import jax
import jax.numpy as jnp
from jax import lax
import numpy as np

D_MODEL = 1024
BATCH = 4
SEQ = 8192
DEPTH = 4
DEC_BATCH = 32
DEC_SEQ = 32
PAST_LEN = 4096

CHUNK = 64
MIX_W = 512
N_BRANCH = 3
GLA_HEADS = 4
GLA_DK = 128
GLA_DV = MIX_W // GLA_HEADS
GLA_QK_W = GLA_HEADS * GLA_DK
GLA_V_W = GLA_HEADS * GLA_DV
GLA_RANK = 16
GLA_TAU = 16.0
FOX_HEADS = 8
FOX_DH = MIX_W // FOX_HEADS
FOX_W = FOX_HEADS * FOX_DH
FOX_QBLOCK = 128
GMLP_GROUPS = 4
GMLP_GDIM = MIX_W // GMLP_GROUPS
GMLP_W = GMLP_GROUPS * GMLP_GDIM
GMLP_CHUNK = 128
D_FF = 3584
N_EXPERTS = 8
TOP_K = 2
N_DENSE = (DEPTH + 1) // 2
N_MOE = DEPTH // 2
ALPHA = (2 * DEPTH) ** 0.25
BETA = (8 * DEPTH) ** -0.25
LN_EPS = 1e-5
RMS_EPS = 1e-6

IN_SIZES = (GLA_QK_W, GLA_QK_W, GLA_V_W, GLA_V_W, GLA_RANK,
            FOX_W, FOX_W, FOX_W, FOX_HEADS,
            GMLP_W, GMLP_W,
            N_BRANCH * D_MODEL)
N_IN = 2 * GLA_QK_W + 2 * GLA_V_W + GLA_RANK + 3 * FOX_W + FOX_HEADS + 2 * GMLP_W + N_BRANCH * D_MODEL

kernel_name = 'hybrid_gla_gmlp_fox_streaming_step'

F32 = jnp.float32


def split_cols(z, sizes):
    offsets = []
    acc = 0
    for s in sizes[:-1]:
        acc += s
        offsets.append(acc)
    return jnp.split(z, offsets, axis=-1)


def layer_norm(x, g, b):
    xf = x.astype(F32)
    mu = jnp.mean(xf, axis=-1, keepdims=True)
    var = jnp.mean(jnp.square(xf - mu), axis=-1, keepdims=True)
    return ((xf - mu) * lax.rsqrt(var + LN_EPS) * g.astype(F32) + b.astype(F32)).astype(x.dtype)


def head_rms_norm(o, g):
    B, T, H, d = o.shape
    o = o * lax.rsqrt(jnp.mean(jnp.square(o), axis=-1, keepdims=True) + RMS_EPS)
    return o.reshape(B, T, H * d) * g.astype(F32)


def gla_chunked(q, k, v, log_a, s0):
    B, T, H, DK = q.shape
    DV = v.shape[-1]
    C = CHUNK if T % CHUNK == 0 else T
    N = T // C
    q = q.astype(F32).reshape(B, N, C, H, DK) * (DK ** -0.5)
    k = k.astype(F32).reshape(B, N, C, H, DK)
    v = v.astype(F32).reshape(B, N, C, H, DV)
    b = jnp.cumsum(log_a.astype(F32).reshape(B, N, C, H, DK), axis=2)
    b_last = b[:, :, -1]
    qe = q * jnp.exp(b)
    ke = k * jnp.exp(-b)
    kd = k * jnp.exp(b_last[:, :, None] - b)
    causal = jnp.tril(jnp.ones((C, C), dtype=bool))
    A = jnp.where(causal, jnp.einsum('bnthd,bnshd->bnhts', qe, ke), 0.0)
    o_intra = jnp.einsum('bnhts,bnshv->bnthv', A, v)
    delta = jnp.einsum('bnshd,bnshv->bnhdv', kd, v)
    decay = jnp.exp(b_last)

    def step(S, inp):
        dec, dl = inp
        return dec[..., None] * S + dl, S

    s_fin, s_start = lax.scan(step, s0.astype(F32),
                              (jnp.moveaxis(decay, 1, 0), jnp.moveaxis(delta, 1, 0)))
    s_start = jnp.moveaxis(s_start, 0, 1)
    o = o_intra + jnp.einsum('bnthd,bnhdv->bnthv', qe, s_start)
    return o.reshape(B, T, H, DV), s_fin


def fox_attention(q, k_all, v_all, logf_all, q_offset):
    B, T, H, DH = q.shape
    L = k_all.shape[1]
    QB = FOX_QBLOCK if T % FOX_QBLOCK == 0 else T
    NB = T // QB
    d_key = jnp.transpose(jnp.cumsum(logf_all.astype(F32), axis=1), (0, 2, 1))
    d_q = d_key[:, :, q_offset:]
    qb = jnp.moveaxis(q.astype(F32).reshape(B, NB, QB, H, DH), 1, 0) * (DH ** -0.5)
    dqb = jnp.moveaxis(d_q.reshape(B, H, NB, QB), 2, 0)
    kf = k_all.astype(F32)
    vf = v_all.astype(F32)
    key_pos = jnp.arange(L)

    def block(args):
        qi, di, i = args
        s = jnp.einsum('bqhd,bkhd->bhqk', qi, kf) + (di[..., None] - d_key[:, :, None, :])
        q_pos = q_offset + i * QB + jnp.arange(QB)
        s = jnp.where(key_pos[None, :] <= q_pos[:, None], s, -jnp.inf)
        p = jax.nn.softmax(s, axis=-1)
        return jnp.einsum('bhqk,bkhd->bqhd', p, vf)

    o = lax.map(block, (qb, dqb, jnp.arange(NB)))
    return jnp.moveaxis(o, 0, 1).reshape(B, T, H, DH)


def gmlp_spatial(u, v, ws, bs):
    B, T, G, c = u.shape
    L = GMLP_CHUNK if T % GMLP_CHUNK == 0 else T
    N = T // L
    w = ws[:, :L, :L] * jnp.tril(jnp.ones((L, L), ws.dtype))
    vr = v.reshape(B, N, L, G, c)
    mixed = jnp.einsum('gts,bnsgc->bntgc', w, vr) + bs[:, :L].T[None, None, :, :, None]
    return u * mixed.reshape(B, T, G, c)


def swiglu(x, wg, wu, wd):
    return (jax.nn.silu(x @ wg) * (x @ wu)) @ wd


def moe_swiglu(x, router, wg, wu, wd):
    logits = (x @ router).astype(F32)
    top_v, top_i = lax.top_k(logits, TOP_K)
    top_w = jax.nn.softmax(top_v, axis=-1)
    combine = jnp.sum(jax.nn.one_hot(top_i, N_EXPERTS, dtype=F32) * top_w[..., None], axis=-2)
    combine = combine.astype(x.dtype)
    y = jnp.zeros_like(x)
    for e in range(N_EXPERTS):
        y = y + combine[..., e:e + 1] * swiglu(x, wg[e], wu[e], wd[e])
    return y


def mixer_sublayer(h, w, l, s0, past):
    B, T, _ = h.shape
    z = h @ w['w_in'][l]
    (g_q, g_k, g_v, g_r, g_a1, f_q, f_k, f_v, f_f, m_u, m_v, gate) = split_cols(z, IN_SIZES)

    log_a = jax.nn.log_sigmoid((g_a1 @ w['w_a2'][l] + w['b_a'][l]).astype(F32)) / GLA_TAU
    o_gla, s_new = gla_chunked(g_q.reshape(B, T, GLA_HEADS, GLA_DK), g_k.reshape(B, T, GLA_HEADS, GLA_DK),
                               g_v.reshape(B, T, GLA_HEADS, GLA_DV), log_a.reshape(B, T, GLA_HEADS, GLA_DK), s0)
    o_gla = head_rms_norm(o_gla, w['gla_norm_g'][l]) * jax.nn.silu(g_r.astype(F32))

    k_new = f_k.reshape(B, T, FOX_HEADS, FOX_DH)
    v_new = f_v.reshape(B, T, FOX_HEADS, FOX_DH)
    lf_new = jax.nn.log_sigmoid((f_f + w['b_f'][l]).astype(F32))
    if past is None:
        k_all, v_all, lf_all, off = k_new, v_new, lf_new, 0
    else:
        ck, cv, cl = past
        k_all = jnp.concatenate([ck, k_new.astype(ck.dtype)], axis=1)
        v_all = jnp.concatenate([cv, v_new.astype(cv.dtype)], axis=1)
        lf_all = jnp.concatenate([cl.astype(F32), lf_new], axis=1)
        off = ck.shape[1]
    o_fox = fox_attention(f_q.reshape(B, T, FOX_HEADS, FOX_DH), k_all, v_all, lf_all, off).reshape(B, T, FOX_W)

    gm_u = jax.nn.gelu(m_u)
    gm_v = layer_norm(jax.nn.gelu(m_v), w['gmlp_norm_g'][l], w['gmlp_norm_b'][l])
    o_gm = gmlp_spatial(gm_u.reshape(B, T, GMLP_GROUPS, GMLP_GDIM), gm_v.reshape(B, T, GMLP_GROUPS, GMLP_GDIM),
                        w['gmlp_ws'][l], w['gmlp_bs'][l]).reshape(B, T, GMLP_W)

    branches = jnp.stack([o_gla.astype(h.dtype), o_gm.astype(h.dtype), o_fox.astype(h.dtype)], axis=2)
    proj = jnp.einsum('btrc,rcd->btrd', branches, w['w_branch'][l])
    gates = jax.nn.sigmoid(gate.reshape(B, T, N_BRANCH, D_MODEL))
    out = jnp.sum(gates * proj, axis=2) @ w['w_out'][l]
    return out, s_new.astype(s0.dtype), k_new, v_new, lf_new.astype(h.dtype), gm_v


def trunk(x, w, gla_s, fox_k, fox_v, fox_lf, keep_gmlp_rows):
    B, T, _ = x.shape
    h = layer_norm(x, w['ln_in_g'], w['ln_in_b'])
    s_out, k_out, v_out, lf_out, gm_out = [], [], [], [], []
    for l in range(DEPTH):
        if gla_s is None:
            s0 = jnp.zeros((B, GLA_HEADS, GLA_DK, GLA_DV), x.dtype)
            past = None
        else:
            s0 = gla_s[l]
            past = (fox_k[l], fox_v[l], fox_lf[l])
        out, s_new, k_new, v_new, lf_new, gm_v = mixer_sublayer(h, w, l, s0, past)
        h = layer_norm(ALPHA * h + out, w['ln_g'][l, 0], w['ln_b'][l, 0])
        if l % 2 == 0:
            j = l // 2
            f = swiglu(h, w['ffn_w_gate'][j], w['ffn_w_up'][j], w['ffn_w_down'][j])
        else:
            j = l // 2
            f = moe_swiglu(h, w['moe_router'][j], w['moe_w_gate'][j], w['moe_w_up'][j], w['moe_w_down'][j])
        h = layer_norm(ALPHA * h + f, w['ln_g'][l, 1], w['ln_b'][l, 1])
        s_out.append(s_new)
        k_out.append(k_new)
        v_out.append(v_new)
        lf_out.append(lf_new)
        if keep_gmlp_rows:
            gm_out.append(gm_v)
    gm_stack = jnp.stack(gm_out) if keep_gmlp_rows else None
    return h, jnp.stack(s_out), jnp.stack(k_out), jnp.stack(v_out), jnp.stack(lf_out), gm_stack


def setup_inputs(seed: int = 0) -> dict:
    key = jax.random.key(seed)
    ks = jax.random.split(key, 32)

    def nrm(k, shape, scale=1.0):
        return jax.random.normal(k, shape, F32) * scale

    return {
        'x_prompt': nrm(ks[0], (BATCH, SEQ, D_MODEL)),
        'x_sample': nrm(ks[1], (DEC_BATCH, DEC_SEQ, D_MODEL)),
        'state_gla': nrm(ks[2], (DEPTH, DEC_BATCH, GLA_HEADS, GLA_DK, GLA_DV), 0.5),
        'cache_fox_k': nrm(ks[3], (DEPTH, DEC_BATCH, PAST_LEN, FOX_HEADS, FOX_DH)),
        'cache_fox_v': nrm(ks[4], (DEPTH, DEC_BATCH, PAST_LEN, FOX_HEADS, FOX_DH)),
        'cache_fox_logf': jax.nn.log_sigmoid(nrm(ks[5], (DEPTH, DEC_BATCH, PAST_LEN, FOX_HEADS)) + 2.0),
        'ln_in_g': 1.0 + nrm(ks[6], (D_MODEL,), 0.02),
        'ln_in_b': nrm(ks[7], (D_MODEL,), 0.02),
        'w_in': nrm(ks[8], (DEPTH, D_MODEL, N_IN), D_MODEL ** -0.5),
        'w_a2': nrm(ks[9], (DEPTH, GLA_RANK, GLA_QK_W), GLA_RANK ** -0.5),
        'b_a': nrm(ks[10], (DEPTH, GLA_QK_W), 0.02),
        'gla_norm_g': 1.0 + nrm(ks[11], (DEPTH, GLA_V_W), 0.02),
        'b_f': 2.0 + nrm(ks[12], (DEPTH, FOX_HEADS), 0.1),
        'gmlp_norm_g': 1.0 + nrm(ks[13], (DEPTH, GMLP_W), 0.02),
        'gmlp_norm_b': nrm(ks[14], (DEPTH, GMLP_W), 0.02),
        'gmlp_ws': nrm(ks[15], (DEPTH, GMLP_GROUPS, GMLP_CHUNK, GMLP_CHUNK), GMLP_CHUNK ** -0.5),
        'gmlp_bs': 1.0 + nrm(ks[16], (DEPTH, GMLP_GROUPS, GMLP_CHUNK), 0.02),
        'w_branch': nrm(ks[17], (DEPTH, N_BRANCH, MIX_W, D_MODEL), BETA * MIX_W ** -0.5),
        'w_out': nrm(ks[18], (DEPTH, D_MODEL, D_MODEL), BETA * D_MODEL ** -0.5),
        'ln_g': 1.0 + nrm(ks[19], (DEPTH, 2, D_MODEL), 0.02),
        'ln_b': nrm(ks[20], (DEPTH, 2, D_MODEL), 0.02),
        'ffn_w_gate': nrm(ks[21], (N_DENSE, D_MODEL, D_FF), D_MODEL ** -0.5),
        'ffn_w_up': nrm(ks[22], (N_DENSE, D_MODEL, D_FF), D_MODEL ** -0.5),
        'ffn_w_down': nrm(ks[23], (N_DENSE, D_FF, D_MODEL), BETA * D_FF ** -0.5),
        'moe_router': nrm(ks[24], (N_MOE, D_MODEL, N_EXPERTS), D_MODEL ** -0.5),
        'moe_w_gate': nrm(ks[25], (N_MOE, N_EXPERTS, D_MODEL, D_FF), D_MODEL ** -0.5),
        'moe_w_up': nrm(ks[26], (N_MOE, N_EXPERTS, D_MODEL, D_FF), D_MODEL ** -0.5),
        'moe_w_down': nrm(ks[27], (N_MOE, N_EXPERTS, D_FF, D_MODEL), BETA * D_FF ** -0.5),
    }


def reference(x_prompt, x_sample, state_gla, cache_fox_k, cache_fox_v, cache_fox_logf,
              ln_in_g, ln_in_b, w_in, w_a2, b_a, gla_norm_g, b_f, gmlp_norm_g, gmlp_norm_b,
              gmlp_ws, gmlp_bs, w_branch, w_out, ln_g, ln_b, ffn_w_gate, ffn_w_up, ffn_w_down,
              moe_router, moe_w_gate, moe_w_up, moe_w_down):
    w = dict(ln_in_g=ln_in_g, ln_in_b=ln_in_b, w_in=w_in, w_a2=w_a2, b_a=b_a, gla_norm_g=gla_norm_g,
             b_f=b_f, gmlp_norm_g=gmlp_norm_g, gmlp_norm_b=gmlp_norm_b, gmlp_ws=gmlp_ws, gmlp_bs=gmlp_bs,
             w_branch=w_branch, w_out=w_out, ln_g=ln_g, ln_b=ln_b, ffn_w_gate=ffn_w_gate,
             ffn_w_up=ffn_w_up, ffn_w_down=ffn_w_down, moe_router=moe_router, moe_w_gate=moe_w_gate,
             moe_w_up=moe_w_up, moe_w_down=moe_w_down)
    y_prompt, gla_p, fk_p, fv_p, flf_p, _ = trunk(x_prompt, w, None, None, None, None, False)
    y_sample, gla_s, fk_s, fv_s, flf_s, gmv_s = trunk(x_sample, w, state_gla, cache_fox_k, cache_fox_v,
                                                      cache_fox_logf, True)
    return (y_prompt, y_sample, gla_p, fk_p, fv_p, flf_p, gla_s, fk_s, fv_s, flf_s, gmv_s)
```

```python
import functools

import jax
import jax.numpy as jnp
from jax import lax
from jax.experimental import pallas as pl
from jax.experimental.pallas import tpu as pltpu

F32 = jnp.float32
BF16 = jnp.bfloat16

GLA_HEADS = 4
GLA_CHUNK = 64
GLA_TAU = 16.0
FOX_HEADS = 8
GMLP_GROUPS = 4
GMLP_CHUNK = 128
TOP_K = 2
LN_EPS = 1e-5
RMS_EPS = 1e-6

V7X_VMEM_BYTES = 64 * 1024 * 1024
VMEM_LIMIT_BYTES = V7X_VMEM_BYTES * 3 // 4
LANES = 128


def _params(*semantics):
    return pltpu.CompilerParams(dimension_semantics=semantics, vmem_limit_bytes=VMEM_LIMIT_BYTES)


def _const_spec(shape):
    zeros = (0,) * len(shape)
    return pl.BlockSpec(shape, lambda *_: zeros)


def _dot(a, b):
    return jnp.dot(a, b, preferred_element_type=F32)


def _dot_nt(a, b):
    return lax.dot_general(a, b, (((1,), (1,)), ((), ())), preferred_element_type=F32)


def _dot_tn(a, b):
    return lax.dot_general(a, b, (((0,), (0,)), ((), ())), preferred_element_type=F32)


def _layer_norm(x, g, b):
    mu = jnp.mean(x, axis=-1, keepdims=True)
    xc = x - mu
    var = jnp.mean(xc * xc, axis=-1, keepdims=True)
    return xc * lax.rsqrt(var + LN_EPS) * g + b


def _log_sigmoid(x):
    return -(jnp.maximum(-x, 0.0) + jnp.log1p(jnp.exp(-jnp.abs(x))))


def _silu(x):
    return x * jax.nn.sigmoid(x)


def _split3(x):
    hi = x.astype(BF16)
    r1 = x - hi.astype(F32)
    mid = r1.astype(BF16)
    lo = (r1 - mid.astype(F32)).astype(BF16)
    return hi, mid, lo


def _tri_cumsum(tri_bf, x):
    hi, mid, lo = _split3(x)
    return _dot(tri_bf, hi) + _dot(tri_bf, mid) + _dot(tri_bf, lo)


def _lower_tri(n):
    row = lax.broadcasted_iota(jnp.int32, (n, n), 0)
    col = lax.broadcasted_iota(jnp.int32, (n, n), 1)
    return row >= col


def _row_tile(n, cap):
    t = min(n, cap)
    assert n % t == 0, (n, cap)
    return t


def _ln_in_kernel(x_ref, g_ref, b_ref, o32_ref, o16_ref):
    y = _layer_norm(x_ref[...], g_ref[...], b_ref[...])
    o32_ref[...] = y
    o16_ref[...] = y.astype(BF16)


def _ln_in(x2, g, b):
    n, d = x2.shape
    tm = _row_tile(n, 512)
    row = pl.BlockSpec((tm, d), lambda i: (i, 0))
    return pl.pallas_call(
        _ln_in_kernel,
        grid=(n // tm,),
        in_specs=[row, _const_spec((1, d)), _const_spec((1, d))],
        out_specs=[row, row],
        out_shape=[jax.ShapeDtypeStruct((n, d), F32), jax.ShapeDtypeStruct((n, d), BF16)],
        compiler_params=_params("parallel"),
        name="ln_in",
    )(x2, g.reshape(1, d), b.reshape(1, d))


def _gla_kernel(h_ref, wz_ref, wa1_ref, wa2_ref, ba_ref, gn_ref, s0_ref, o_ref, sfin_ref, s_scr,
                *, chunk, n_chunk, dk, dv):
    t = pl.program_id(1)

    @pl.when(t == 0)
    def _():
        s_scr[...] = s0_ref[...]

    qk_w = GLA_HEADS * dk
    v_w = GLA_HEADS * dv
    x = h_ref[...]
    z = _dot(x, wz_ref[...])
    a1 = _dot(x, wa1_ref[...])
    log_a = _log_sigmoid(_dot(a1.astype(BF16), wa2_ref[...]) + ba_ref[...]) * (1.0 / GLA_TAU)

    tri = _lower_tri(chunk)
    tri_bf = jnp.where(tri, 1.0, 0.0).astype(BF16)
    for c in range(n_chunk):
        rows = slice(c * chunk, (c + 1) * chunk)
        b = _tri_cumsum(tri_bf, log_a[rows])
        b_last = b[chunk - 1:chunk, :]
        q = z[rows, 0:qk_w]
        k = z[rows, qk_w:2 * qk_w]
        v = z[rows, 2 * qk_w:2 * qk_w + v_w]
        r = z[rows, 2 * qk_w + v_w:2 * qk_w + 2 * v_w]
        qe = (q * (dk ** -0.5)) * jnp.exp(b)
        ke = k * jnp.exp(-b)
        kd = k * jnp.exp(b_last - b)
        decay = jnp.exp(b_last)
        for hd in range(GLA_HEADS):
            kl = slice(hd * dk, (hd + 1) * dk)
            vl = slice(hd * dv, (hd + 1) * dv)
            qe_h = qe[:, kl].astype(BF16)
            v_h = v[:, vl].astype(BF16)
            a = jnp.where(tri, _dot_nt(qe_h, ke[:, kl].astype(BF16)), 0.0)
            s_t = s_scr[hd]
            o = _dot(a.astype(BF16), v_h) + _dot_nt(qe_h, s_t.astype(BF16))
            s_scr[hd] = s_t * decay[:, kl] + _dot_tn(v_h, kd[:, kl].astype(BF16))
            o = o * lax.rsqrt(jnp.mean(o * o, axis=-1, keepdims=True) + RMS_EPS) * gn_ref[:, vl]
            o_ref[rows, vl] = (o * _silu(r[:, vl])).astype(o_ref.dtype)

    @pl.when(t == pl.num_programs(1) - 1)
    def _():
        sfin_ref[...] = s_scr[...]


def _gla(h16, wz, wa1, wa2, ba, gn, s0_t):
    bsz, seq, d = h16.shape
    _, heads, dv, dk = s0_t.shape
    chunk = GLA_CHUNK if seq % GLA_CHUNK == 0 else seq
    tc = _row_tile(seq, 512)
    kern = functools.partial(_gla_kernel, chunk=chunk, n_chunk=tc // chunk, dk=dk, dv=dv)
    state = pl.BlockSpec((None, heads, dv, dk), lambda b, t: (b, 0, 0, 0))
    return pl.pallas_call(
        kern,
        grid=(bsz, seq // tc),
        in_specs=[pl.BlockSpec((None, tc, d), lambda b, t: (b, t, 0)),
                  _const_spec(wz.shape), _const_spec(wa1.shape), _const_spec(wa2.shape),
                  _const_spec(ba.shape), _const_spec(gn.shape), state],
        out_specs=[pl.BlockSpec((None, tc, heads * dv), lambda b, t: (b, t, 0)), state],
        out_shape=[jax.ShapeDtypeStruct((bsz, seq, heads * dv), BF16),
                   jax.ShapeDtypeStruct(s0_t.shape, F32)],
        scratch_shapes=[pltpu.VMEM((heads, dv, dk), F32)],
        compiler_params=_params("parallel", "arbitrary"),
        name="gla",
    )(h16, wz, wa1, wa2, ba, gn, s0_t)


def _gmlp_kernel(h_ref, w_ref, g_ref, b_ref, ws_ref, bs_ref, o_ref, *v_ref, chunk, n_chunk, width):
    x = h_ref[...]
    z = _dot(x, w_ref[...])
    u = jax.nn.gelu(z[:, :width])
    v = _layer_norm(jax.nn.gelu(z[:, width:]), g_ref[...], b_ref[...])
    if v_ref:
        v_ref[0][...] = v
    gdim = width // GMLP_GROUPS
    tri = _lower_tri(chunk)
    for g in range(GMLP_GROUPS):
        w_g = jnp.where(tri, ws_ref[g], 0.0).astype(BF16)
        bias = bs_ref[:, g:g + 1]
        cols = slice(g * gdim, (g + 1) * gdim)
        for n in range(n_chunk):
            rows = slice(n * chunk, (n + 1) * chunk)
            mixed = _dot(w_g, v[rows, cols].astype(BF16)) + bias
            o_ref[rows, cols] = (u[rows, cols] * mixed).astype(o_ref.dtype)


def _gmlp(h16, w, g, b, ws, bs_t, keep_v):
    bsz, seq, d = h16.shape
    width = g.shape[-1]
    chunk = ws.shape[-1]
    tc = _row_tile(seq, 512)
    kern = functools.partial(_gmlp_kernel, chunk=chunk, n_chunk=tc // chunk, width=width)
    tile = pl.BlockSpec((None, tc, width), lambda i, t: (i, t, 0))
    out_specs = [tile]
    out_shape = [jax.ShapeDtypeStruct((bsz, seq, width), BF16)]
    if keep_v:
        out_specs.append(tile)
        out_shape.append(jax.ShapeDtypeStruct((bsz, seq, width), F32))
    return pl.pallas_call(
        kern,
        grid=(bsz, seq // tc),
        in_specs=[pl.BlockSpec((None, tc, d), lambda i, t: (i, t, 0)),
                  _const_spec(w.shape), _const_spec(g.shape), _const_spec(b.shape),
                  _const_spec(ws.shape), _const_spec(bs_t.shape)],
        out_specs=out_specs,
        out_shape=out_shape,
        compiler_params=_params("parallel", "parallel"),
        name="gmlp",
    )(h16, w, g, b, ws, bs_t)


def _fox_pack(dcum, zq, zk, zv, q_ref, k_ref, v_ref, *, dh):
    tm = dcum.shape[0]
    hi = dcum.astype(BF16).astype(F32)
    r1 = dcum - hi
    mid = r1.astype(BF16).astype(F32)
    lo = (r1 - mid).astype(BF16).astype(F32)
    lane = lax.broadcasted_iota(jnp.int32, (tm, LANES), 1)
    for h in range(FOX_HEADS):
        cols = slice((h // 2) * LANES, (h // 2 + 1) * LANES)
        odd = h % 2
        data = (lane >= dh) if odd else (lane < dh)
        slot = lane - (0 if odd else dh)
        d_hi = jnp.broadcast_to(hi[:, h:h + 1], (tm, LANES))
        d_mid = jnp.broadcast_to(mid[:, h:h + 1], (tm, LANES))
        d_lo = jnp.broadcast_to(lo[:, h:h + 1], (tm, LANES))
        ones_first = jnp.where(slot < 0, 0.0, jnp.where(slot < 3, 1.0, 0.0))
        ones_second = jnp.where(slot < 3, 0.0, jnp.where(slot < 6, 1.0, 0.0))
        k_extra = jnp.where(slot == 3, -d_hi, jnp.where(slot == 4, -d_mid, jnp.where(slot == 5, -d_lo, ones_first)))
        k_ref[h] = jnp.where(data, zk[:, cols], k_extra).astype(k_ref.dtype)
        v_ref[h] = jnp.where(data, zv[:, cols], 0.0).astype(v_ref.dtype)
        if q_ref is not None:
            q_extra = jnp.where(slot == 0, d_hi, jnp.where(slot == 1, d_mid, jnp.where(slot == 2, d_lo, ones_second)))
            q_ref[h] = jnp.where(data, zq[:, cols] * (dh ** -0.5), q_extra).astype(q_ref.dtype)


def _fox_proj_kernel(h_ref, w_ref, wf_ref, bf_ref, d0_ref, kout_ref, vout_ref, lf_ref,
                     q_ref, k_ref, v_ref, d_scr, *, width):
    t = pl.program_id(1)

    @pl.when(t == 0)
    def _():
        d_scr[...] = d0_ref[...]

    tm = h_ref.shape[0]
    x = h_ref[...]
    z = _dot(x, w_ref[...])
    zq, zk, zv = z[:, :width], z[:, width:2 * width], z[:, 2 * width:]
    kout_ref[...] = zk
    vout_ref[...] = zv
    lf = _log_sigmoid(_dot(x, wf_ref[...]) + bf_ref[...])
    lane = lax.broadcasted_iota(jnp.int32, lf.shape, 1)
    lf = jnp.where(lane < FOX_HEADS, lf, 0.0)
    lf_ref[...] = lf[:, :FOX_HEADS]
    tri_bf = jnp.where(_lower_tri(tm), 1.0, 0.0).astype(BF16)
    dcum = _tri_cumsum(tri_bf, lf) + d_scr[...]
    d_scr[...] = dcum[tm - 1:tm, :]
    _fox_pack(dcum, zq, zk, zv, q_ref, k_ref, v_ref, dh=width // FOX_HEADS)


def _fox_proj(h16, w, wf, bf, d0):
    bsz, seq, d = h16.shape
    width = w.shape[1] // 3
    tm = _row_tile(seq, 512)
    kern = functools.partial(_fox_proj_kernel, width=width)
    tile = pl.BlockSpec((None, tm, width), lambda b, t: (b, t, 0))
    packed = pl.BlockSpec((None, FOX_HEADS, tm, LANES), lambda b, t: (b, 0, t, 0))
    packed_shape = jax.ShapeDtypeStruct((bsz, FOX_HEADS, seq, LANES), BF16)
    return pl.pallas_call(
        kern,
        grid=(bsz, seq // tm),
        in_specs=[pl.BlockSpec((None, tm, d), lambda b, t: (b, t, 0)),
                  _const_spec(w.shape), _const_spec(wf.shape), _const_spec(bf.shape),
                  pl.BlockSpec((None, 1, LANES), lambda b, t: (b, 0, 0))],
        out_specs=[tile, tile, pl.BlockSpec((None, tm, FOX_HEADS), lambda b, t: (b, t, 0)),
                   packed, packed, packed],
        out_shape=[jax.ShapeDtypeStruct((bsz, seq, width), F32), jax.ShapeDtypeStruct((bsz, seq, width), F32),
                   jax.ShapeDtypeStruct((bsz, seq, FOX_HEADS), F32),
                   packed_shape, packed_shape, packed_shape],
        scratch_shapes=[pltpu.VMEM((1, LANES), F32)],
        compiler_params=_params("parallel", "arbitrary"),
        name="fox_proj",
    )(h16, w, wf, bf, d0)


def _fox_cache_kernel(ck_ref, cv_ref, cl_ref, k_ref, v_ref, dend_ref, d_scr, *, width):
    t = pl.program_id(1)

    @pl.when(t == 0)
    def _():
        d_scr[...] = jnp.zeros_like(d_scr)

    tm = ck_ref.shape[0]
    tri_bf = jnp.where(_lower_tri(tm), 1.0, 0.0).astype(BF16)
    dcum = _tri_cumsum(tri_bf, cl_ref[...]) + d_scr[...]
    d_scr[...] = dcum[tm - 1:tm, :]
    _fox_pack(dcum, None, ck_ref[...], cv_ref[...], None, k_ref, v_ref, dh=width // FOX_HEADS)

    @pl.when(t == pl.num_programs(1) - 1)
    def _():
        lane = lax.broadcasted_iota(jnp.int32, (1, LANES), 1)
        row = jnp.zeros((1, LANES), F32)
        for h in range(FOX_HEADS):
            row = jnp.where(lane == h, dcum[tm - 1:tm, h:h + 1], row)
        dend_ref[...] = row


def _fox_cache(ck, cv, cl):
    bsz, past, width = ck.shape
    tm = _row_tile(past, 512)
    kern = functools.partial(_fox_cache_kernel, width=width)
    tile = pl.BlockSpec((None, tm, width), lambda b, t: (b, t, 0))
    packed = pl.BlockSpec((None, FOX_HEADS, tm, LANES), lambda b, t: (b, 0, t, 0))
    packed_shape = jax.ShapeDtypeStruct((bsz, FOX_HEADS, past, LANES), BF16)
    return pl.pallas_call(
        kern,
        grid=(bsz, past // tm),
        in_specs=[tile, tile, pl.BlockSpec((None, tm, FOX_HEADS), lambda b, t: (b, t, 0))],
        out_specs=[packed, packed, pl.BlockSpec((None, 1, LANES), lambda b, t: (b, 0, 0))],
        out_shape=[packed_shape, packed_shape, jax.ShapeDtypeStruct((bsz, 1, LANES), F32)],
        scratch_shapes=[pltpu.VMEM((1, FOX_HEADS), F32)],
        compiler_params=_params("parallel", "arbitrary"),
        name="fox_cache",
    )(ck, cv, cl)


def _fox_attn_kernel(q_ref, kp_ref, vp_ref, kn_ref, vn_ref, o_ref, m_scr, l_scr, acc_scr,
                     *, tk, full_base, full_per_q):
    tq = q_ref.shape[1]
    n_full = full_base + pl.program_id(2) * full_per_q
    causal = _lower_tri(tq)
    out = None
    for hh in range(2):
        q = q_ref[hh]
        m_scr[...] = jnp.full_like(m_scr, -jnp.inf)
        l_scr[...] = jnp.zeros_like(l_scr)
        acc_scr[...] = jnp.zeros_like(acc_scr)

        def visit(k_blk, v_blk, masked, q=q):
            s = _dot_nt(q, k_blk)
            if masked:
                s = jnp.where(causal, s, -jnp.inf)
            m_prev = m_scr[...]
            m_new = jnp.maximum(m_prev, jnp.max(s, axis=-1, keepdims=True))
            alpha = jnp.exp(m_prev - m_new)
            p = jnp.exp(s - m_new)
            l_scr[...] = alpha * l_scr[...] + jnp.sum(p, axis=-1, keepdims=True)
            acc_scr[...] = alpha * acc_scr[...] + _dot(p.astype(BF16), v_blk)
            m_scr[...] = m_new

        def body(i, carry, hh=hh, visit=visit):
            rows = pl.ds(pl.multiple_of(i * tk, tk), tk)
            visit(kp_ref[hh, rows, :], vp_ref[hh, rows, :], False)
            return carry

        lax.fori_loop(0, n_full, body, 0)
        visit(kn_ref[hh], vn_ref[hh], True)
        o_h = acc_scr[...] / l_scr[...]
        out = o_h if out is None else out + o_h
    o_ref[...] = out.astype(o_ref.dtype)


def _fox_attn(q, kp, vp, kn, vn, *, tq, tk):
    bsz, heads, seq, _ = q.shape
    past = kp.shape[2]
    off = past if kp is not kn else 0
    nq = seq // tq
    assert off % tk == 0 and (nq == 1 or tq % tk == 0), (off, tq, tk)
    kern = functools.partial(_fox_attn_kernel, tk=tk, full_base=off // tk, full_per_q=tq // tk)
    new = pl.BlockSpec((None, 2, tq, LANES), lambda b, j, i: (b, j, i, 0))
    old = pl.BlockSpec((None, 2, past, LANES), lambda b, j, i: (b, j, 0, 0))
    return pl.pallas_call(
        kern,
        grid=(bsz, heads // 2, nq),
        in_specs=[new, old, old, new, new],
        out_specs=pl.BlockSpec((None, tq, LANES), lambda b, j, i: (b, i, j)),
        out_shape=jax.ShapeDtypeStruct((bsz, seq, heads // 2 * LANES), BF16),
        scratch_shapes=[pltpu.VMEM((tq, 1), F32), pltpu.VMEM((tq, 1), F32), pltpu.VMEM((tq, LANES), F32)],
        compiler_params=_params("parallel", "parallel", "arbitrary"),
        name="fox_attn",
    )(q, kp, vp, kn, vn)


def _merge_kernel(h32_ref, h16_ref, og_ref, om_ref, of_ref, wg_ref, wb_ref, wo_ref, lg_ref, lb_ref,
                  o32_ref, o16_ref, *, alpha):
    d = h32_ref.shape[1]
    x = h16_ref[...]
    m = None
    for r, o_ref in enumerate((og_ref, om_ref, of_ref)):
        gate = jax.nn.sigmoid(_dot(x, wg_ref[:, r * d:(r + 1) * d]))
        term = gate * _dot(o_ref[...], wb_ref[r])
        m = term if m is None else m + term
    y = alpha * h32_ref[...] + _dot(m.astype(BF16), wo_ref[...])
    y = _layer_norm(y, lg_ref[...], lb_ref[...])
    o32_ref[...] = y
    o16_ref[...] = y.astype(BF16)


def _merge(h32, h16, o_gla, o_gm, o_fox, wg, wb, wo, lg, lb, alpha):
    n, d = h32.shape
    w = o_gla.shape[1]
    tm = _row_tile(n, 256)
    row = pl.BlockSpec((tm, d), lambda i: (i, 0))
    mix = pl.BlockSpec((tm, w), lambda i: (i, 0))
    return pl.pallas_call(
        functools.partial(_merge_kernel, alpha=alpha),
        grid=(n // tm,),
        in_specs=[row, row, mix, mix, mix, _const_spec(wg.shape), _const_spec(wb.shape),
                  _const_spec(wo.shape), _const_spec(lg.shape), _const_spec(lb.shape)],
        out_specs=[row, row],
        out_shape=[jax.ShapeDtypeStruct((n, d), F32), jax.ShapeDtypeStruct((n, d), BF16)],
        compiler_params=_params("parallel"),
        name="merge",
    )(h32, h16, o_gla, o_gm, o_fox, wg, wb, wo, lg, lb)


def _ffn_kernel(h32_ref, h16_ref, wg_ref, wu_ref, wd_ref, lg_ref, lb_ref, o32_ref, o16_ref, acc_scr, *, alpha):
    j = pl.program_id(1)

    @pl.when(j == 0)
    def _():
        acc_scr[...] = jnp.zeros_like(acc_scr)

    x = h16_ref[...]
    a = _silu(_dot(x, wg_ref[...])) * _dot(x, wu_ref[...])
    acc_scr[...] += _dot(a.astype(BF16), wd_ref[...])

    @pl.when(j == pl.num_programs(1) - 1)
    def _():
        y = _layer_norm(alpha * h32_ref[...] + acc_scr[...], lg_ref[...], lb_ref[...])
        o32_ref[...] = y
        o16_ref[...] = y.astype(BF16)


def _ffn(h32, h16, wg, wu, wd, lg, lb, alpha):
    n, d = h32.shape
    f = wg.shape[1]
    tm = _row_tile(n, 1024)
    tf = _row_tile(f, 512)
    row = pl.BlockSpec((tm, d), lambda i, j: (i, 0))
    return pl.pallas_call(
        functools.partial(_ffn_kernel, alpha=alpha),
        grid=(n // tm, f // tf),
        in_specs=[row, row, pl.BlockSpec((d, tf), lambda i, j: (0, j)), pl.BlockSpec((d, tf), lambda i, j: (0, j)),
                  pl.BlockSpec((tf, d), lambda i, j: (j, 0)), _const_spec(lg.shape), _const_spec(lb.shape)],
        out_specs=[row, row],
        out_shape=[jax.ShapeDtypeStruct((n, d), F32), jax.ShapeDtypeStruct((n, d), BF16)],
        scratch_shapes=[pltpu.VMEM((tm, d), F32)],
        compiler_params=_params("parallel", "arbitrary"),
        name="ffn",
    )(h32, h16, wg, wu, wd, lg, lb)


def _router_kernel(h_ref, w_ref, c_ref, *, n_experts):
    logits = jnp.dot(h_ref[...], w_ref[...], preferred_element_type=F32, precision=lax.Precision.HIGHEST)
    lane = lax.broadcasted_iota(jnp.int32, logits.shape, 1)
    logits = jnp.where(lane < n_experts, logits, -jnp.inf)
    v1 = jnp.max(logits, axis=-1, keepdims=True)
    i1 = jnp.min(jnp.where(logits == v1, lane, LANES), axis=-1, keepdims=True)
    rest = jnp.where(lane == i1, -jnp.inf, logits)
    v2 = jnp.max(rest, axis=-1, keepdims=True)
    i2 = jnp.min(jnp.where(rest == v2, lane, LANES), axis=-1, keepdims=True)
    e2 = jnp.exp(v2 - v1)
    w1 = 1.0 / (1.0 + e2)
    w2 = e2 / (1.0 + e2)
    c_ref[...] = jnp.where(lane == i1, w1, jnp.where(lane == i2, w2, 0.0))


def _router(h32, w_router):
    n, d = h32.shape
    n_experts = w_router.shape[1]
    w_pad = jnp.pad(w_router, ((0, 0), (0, LANES - n_experts)))
    tm = _row_tile(n, 512)
    return pl.pallas_call(
        functools.partial(_router_kernel, n_experts=n_experts),
        grid=(n // tm,),
        in_specs=[pl.BlockSpec((tm, d), lambda i: (i, 0)), _const_spec(w_pad.shape)],
        out_specs=pl.BlockSpec((tm, LANES), lambda i: (i, 0)),
        out_shape=jax.ShapeDtypeStruct((n, LANES), F32),
        compiler_params=_params("parallel"),
        name="router",
    )(h32, w_pad)


def _moe_kernel(h32_ref, h16_ref, c_ref, wg_ref, wu_ref, wd_ref, lg_ref, lb_ref, o32_ref, o16_ref, acc_scr,
                *, alpha):
    e = pl.program_id(1)
    j = pl.program_id(2)

    @pl.when((e == 0) & (j == 0))
    def _():
        acc_scr[...] = jnp.zeros_like(acc_scr)

    comb = c_ref[...]
    lane = lax.broadcasted_iota(jnp.int32, comb.shape, 1)
    c_e = jnp.sum(jnp.where(lane == e, comb, 0.0), axis=-1, keepdims=True)
    x = h16_ref[...]
    a = c_e * (_silu(_dot(x, wg_ref[...])) * _dot(x, wu_ref[...]))
    acc_scr[...] += _dot(a.astype(BF16), wd_ref[...])

    @pl.when((e == pl.num_programs(1) - 1) & (j == pl.num_programs(2) - 1))
    def _():
        y = _layer_norm(alpha * h32_ref[...] + acc_scr[...], lg_ref[...], lb_ref[...])
        o32_ref[...] = y
        o16_ref[...] = y.astype(BF16)


def _moe(h32, h16, w_router, wg, wu, wd, lg, lb, alpha):
    n, d = h32.shape
    n_experts, _, f = wg.shape
    comb = _router(h32, w_router)
    tm = _row_tile(n, 1024)
    tf = _row_tile(f, 512)
    row = pl.BlockSpec((tm, d), lambda i, e, j: (i, 0))
    return pl.pallas_call(
        functools.partial(_moe_kernel, alpha=alpha),
        grid=(n // tm, n_experts, f // tf),
        in_specs=[row, row, pl.BlockSpec((tm, LANES), lambda i, e, j: (i, 0)),
                  pl.BlockSpec((None, d, tf), lambda i, e, j: (e, 0, j)),
                  pl.BlockSpec((None, d, tf), lambda i, e, j: (e, 0, j)),
                  pl.BlockSpec((None, tf, d), lambda i, e, j: (e, j, 0)),
                  _const_spec(lg.shape), _const_spec(lb.shape)],
        out_specs=[row, row],
        out_shape=[jax.ShapeDtypeStruct((n, d), F32), jax.ShapeDtypeStruct((n, d), BF16)],
        scratch_shapes=[pltpu.VMEM((tm, d), F32)],
        compiler_params=_params("parallel", "arbitrary", "arbitrary"),
        name="moe",
    )(h32, h16, comb, wg, wu, wd, lg, lb)


def _layer_weights(w, l, d, mix_w):
    rank = w['w_a2'].shape[1]
    qk_w = w['w_a2'].shape[2]
    sizes = (qk_w, qk_w, mix_w, mix_w, rank, mix_w, mix_w, mix_w, FOX_HEADS, mix_w, mix_w, 3 * d)
    offs = [0]
    for s in sizes:
        offs.append(offs[-1] + s)
    w_in = w['w_in'][l]
    col = lambda a, b: w_in[:, offs[a]:offs[b]]
    pad_cols = lambda x: jnp.pad(x, ((0, 0), (0, LANES - x.shape[1])))
    return dict(
        gla_wz=col(0, 4).astype(BF16),
        gla_wa1=pad_cols(col(4, 5)).astype(BF16),
        gla_wa2=jnp.pad(w['w_a2'][l], ((0, LANES - rank), (0, 0))).astype(BF16),
        gla_ba=w['b_a'][l].reshape(1, -1),
        gla_gn=w['gla_norm_g'][l].reshape(1, -1),
        fox_w=col(5, 8).astype(BF16),
        fox_wf=pad_cols(col(8, 9)).astype(BF16),
        fox_bf=pad_cols(w['b_f'][l].reshape(1, -1)),
        gm_w=col(9, 11).astype(BF16),
        gm_g=w['gmlp_norm_g'][l].reshape(1, -1),
        gm_b=w['gmlp_norm_b'][l].reshape(1, -1),
        gate_w=col(11, 12).astype(BF16),
        branch_w=w['w_branch'][l].astype(BF16),
        out_w=w['w_out'][l].astype(BF16),
    )


def _trunk(x, w, gla_s, fox_k, fox_v, fox_lf, keep_gmlp_rows):
    bsz, seq, d = x.shape
    depth = w['w_in'].shape[0]
    mix_w = w['gla_norm_g'].shape[1]
    alpha = (2 * depth) ** 0.25
    n = bsz * seq
    dk = w['w_a2'].shape[2] // GLA_HEADS
    dv = mix_w // GLA_HEADS
    gm_chunk = GMLP_CHUNK if seq % GMLP_CHUNK == 0 else seq
    tq = _row_tile(seq, 512)

    h32, h16 = _ln_in(x.reshape(n, d), w['ln_in_g'], w['ln_in_b'])
    s_out, k_out, v_out, lf_out, gm_out = [], [], [], [], []
    for l in range(depth):
        lw = _layer_weights(w, l, d, mix_w)
        h16_seq = h16.reshape(bsz, seq, d)

        if gla_s is None:
            s0_t = jnp.zeros((bsz, GLA_HEADS, dv, dk), F32)
        else:
            s0_t = jnp.swapaxes(gla_s[l], -1, -2)
        o_gla, s_t = _gla(h16_seq, lw['gla_wz'], lw['gla_wa1'], lw['gla_wa2'], lw['gla_ba'], lw['gla_gn'], s0_t)
        s_out.append(jnp.swapaxes(s_t, -1, -2))

        gm = _gmlp(h16_seq, lw['gm_w'], lw['gm_g'], lw['gm_b'], w['gmlp_ws'][l][:, :gm_chunk, :gm_chunk],
                   w['gmlp_bs'][l][:, :gm_chunk].T, keep_gmlp_rows)
        o_gm = gm[0]
        if keep_gmlp_rows:
            gm_out.append(gm[1])

        if fox_k is None:
            d0 = jnp.zeros((bsz, 1, LANES), F32)
        else:
            past = fox_k.shape[2]
            kp, vp, d0 = _fox_cache(fox_k[l].reshape(bsz, past, mix_w), fox_v[l].reshape(bsz, past, mix_w), fox_lf[l])
        k_new, v_new, lf_new, qn, kn, vn = _fox_proj(h16_seq, lw['fox_w'], lw['fox_wf'], lw['fox_bf'], d0)
        if fox_k is None:
            kp, vp = kn, vn
        o_fox = _fox_attn(qn, kp, vp, kn, vn, tq=tq, tk=tq if fox_k is None else _row_tile(past, 512))
        k_out.append(k_new.reshape(bsz, seq, FOX_HEADS, mix_w // FOX_HEADS))
        v_out.append(v_new.reshape(bsz, seq, FOX_HEADS, mix_w // FOX_HEADS))
        lf_out.append(lf_new)

        h32, h16 = _merge(h32, h16, o_gla.reshape(n, mix_w), o_gm.reshape(n, mix_w), o_fox.reshape(n, mix_w),
                          lw['gate_w'], lw['branch_w'], lw['out_w'],
                          w['ln_g'][l, 0].reshape(1, d), w['ln_b'][l, 0].reshape(1, d), alpha)
        lg, lb = w['ln_g'][l, 1].reshape(1, d), w['ln_b'][l, 1].reshape(1, d)
        j = l // 2
        if l % 2 == 0:
            h32, h16 = _ffn(h32, h16, w['ffn_w_gate'][j].astype(BF16), w['ffn_w_up'][j].astype(BF16),
                            w['ffn_w_down'][j].astype(BF16), lg, lb, alpha)
        else:
            h32, h16 = _moe(h32, h16, w['moe_router'][j], w['moe_w_gate'][j].astype(BF16),
                            w['moe_w_up'][j].astype(BF16), w['moe_w_down'][j].astype(BF16), lg, lb, alpha)
    gm_stack = jnp.stack(gm_out) if keep_gmlp_rows else None
    return (h32.reshape(bsz, seq, d), jnp.stack(s_out), jnp.stack(k_out), jnp.stack(v_out), jnp.stack(lf_out),
            gm_stack)


def kernel(x_prompt, x_sample, state_gla, cache_fox_k, cache_fox_v, cache_fox_logf, ln_in_g, ln_in_b, w_in, w_a2, b_a, gla_norm_g, b_f, gmlp_norm_g, gmlp_norm_b, gmlp_ws, gmlp_bs, w_branch, w_out, ln_g, ln_b, ffn_w_gate, ffn_w_up, ffn_w_down, moe_router, moe_w_gate, moe_w_up, moe_w_down):
    w = dict(ln_in_g=ln_in_g, ln_in_b=ln_in_b, w_in=w_in, w_a2=w_a2, b_a=b_a, gla_norm_g=gla_norm_g,
             b_f=b_f, gmlp_norm_g=gmlp_norm_g, gmlp_norm_b=gmlp_norm_b, gmlp_ws=gmlp_ws, gmlp_bs=gmlp_bs,
             w_branch=w_branch, w_out=w_out, ln_g=ln_g, ln_b=ln_b, ffn_w_gate=ffn_w_gate,
             ffn_w_up=ffn_w_up, ffn_w_down=ffn_w_down, moe_router=moe_router, moe_w_gate=moe_w_gate,
             moe_w_up=moe_w_up, moe_w_down=moe_w_down)
    y_p, gla_p, fk_p, fv_p, flf_p, _ = _trunk(x_prompt, w, None, None, None, None, False)
    y_s, gla_s, fk_s, fv_s, flf_s, gmv_s = _trunk(x_sample, w, state_gla, cache_fox_k, cache_fox_v,
                                                  cache_fox_logf, True)
    return (y_p, y_s, gla_p, fk_p, fv_p, flf_p, gla_s, fk_s, fv_s, flf_s, gmv_s)
```

```python
import functools

import jax
import jax.numpy as jnp
from jax import lax
from jax.experimental import pallas as pl
from jax.experimental.pallas import tpu as pltpu

F32 = jnp.float32
BF16 = jnp.bfloat16

GLA_HEADS = 4
GLA_CHUNK = 64
GLA_TAU = 16.0
FOX_HEADS = 8
GMLP_GROUPS = 4
GMLP_CHUNK = 128
TOP_K = 2
ATTN_GROUP = 4
LN_EPS = 1e-5
RMS_EPS = 1e-6

V7X_VMEM_BYTES = 64 * 1024 * 1024
VMEM_LIMIT_BYTES = V7X_VMEM_BYTES * 3 // 4
LANES = 128


def _params(*semantics, flags=None):
    return pltpu.CompilerParams(dimension_semantics=semantics, vmem_limit_bytes=VMEM_LIMIT_BYTES, flags=flags)


def _const_spec(shape):
    zeros = (0,) * len(shape)
    return pl.BlockSpec(shape, lambda *_: zeros)


def _dot(a, b):
    return jnp.dot(a, b, preferred_element_type=F32)


def _dot_nt(a, b):
    return lax.dot_general(a, b, (((1,), (1,)), ((), ())), preferred_element_type=F32)


def _dot_tn(a, b):
    return lax.dot_general(a, b, (((0,), (0,)), ((), ())), preferred_element_type=F32)


def _layer_norm(x, g, b):
    mu = jnp.mean(x, axis=-1, keepdims=True)
    xc = x - mu
    var = jnp.mean(xc * xc, axis=-1, keepdims=True)
    return xc * lax.rsqrt(var + LN_EPS) * g + b


def _log_sigmoid(x):
    return -(jnp.maximum(-x, 0.0) + jnp.log1p(jnp.exp(-jnp.abs(x))))


def _silu(x):
    return x * jax.nn.sigmoid(x)


def _split3(x):
    hi = x.astype(BF16)
    r1 = x - hi.astype(F32)
    mid = r1.astype(BF16)
    lo = (r1 - mid.astype(F32)).astype(BF16)
    return hi, mid, lo


def _tri_cumsum(tri_bf, x):
    hi, mid, lo = _split3(x)
    return _dot(tri_bf, hi) + _dot(tri_bf, mid) + _dot(tri_bf, lo)


def _lower_tri(n):
    row = lax.broadcasted_iota(jnp.int32, (n, n), 0)
    col = lax.broadcasted_iota(jnp.int32, (n, n), 1)
    return row >= col


def _row_tile(n, cap):
    t = min(n, cap)
    assert n % t == 0, (n, cap)
    return t


def _ln_in_kernel(x_ref, g_ref, b_ref, o32_ref, o16_ref):
    y = _layer_norm(x_ref[...], g_ref[...], b_ref[...])
    o32_ref[...] = y
    o16_ref[...] = y.astype(BF16)


def _ln_in(x2, g, b):
    n, d = x2.shape
    tm = _row_tile(n, 512)
    row = pl.BlockSpec((tm, d), lambda i: (i, 0))
    return pl.pallas_call(
        _ln_in_kernel,
        grid=(n // tm,),
        in_specs=[row, _const_spec((1, d)), _const_spec((1, d))],
        out_specs=[row, row],
        out_shape=[jax.ShapeDtypeStruct((n, d), F32), jax.ShapeDtypeStruct((n, d), BF16)],
        compiler_params=_params("parallel"),
        name="ln_in",
    )(x2, g.reshape(1, d), b.reshape(1, d))


def _gla_kernel(h_ref, wz_ref, wa1_ref, wa2_ref, ba_ref, gn_ref, s0_ref, o_ref, sfin_ref, s_scr,
                *, chunk, n_chunk, dk, dv):
    t = pl.program_id(1)

    @pl.when(t == 0)
    def _():
        s_scr[...] = s0_ref[...]

    qk_w = GLA_HEADS * dk
    v_w = GLA_HEADS * dv
    x = h_ref[...]
    z = _dot(x, wz_ref[...])
    a1 = _dot(x, wa1_ref[...])
    log_a = _log_sigmoid(_dot(a1.astype(BF16), wa2_ref[...]) + ba_ref[...]) * (1.0 / GLA_TAU)

    tri = _lower_tri(chunk)
    tri_bf = jnp.where(tri, 1.0, 0.0).astype(BF16)
    for c in range(n_chunk):
        rows = slice(c * chunk, (c + 1) * chunk)
        b = _tri_cumsum(tri_bf, log_a[rows])
        b_last = b[chunk - 1:chunk, :]
        q = z[rows, 0:qk_w]
        k = z[rows, qk_w:2 * qk_w]
        v = z[rows, 2 * qk_w:2 * qk_w + v_w]
        r = z[rows, 2 * qk_w + v_w:2 * qk_w + 2 * v_w]
        qe = (q * (dk ** -0.5)) * jnp.exp(b)
        ke = k * jnp.exp(-b)
        kd = k * jnp.exp(b_last - b)
        decay = jnp.exp(b_last)
        for hd in range(GLA_HEADS):
            kl = slice(hd * dk, (hd + 1) * dk)
            vl = slice(hd * dv, (hd + 1) * dv)
            qe_h = qe[:, kl].astype(BF16)
            v_h = v[:, vl].astype(BF16)
            a = jnp.where(tri, _dot_nt(qe_h, ke[:, kl].astype(BF16)), 0.0)
            s_t = s_scr[hd]
            o = _dot(a.astype(BF16), v_h) + _dot_nt(qe_h, s_t.astype(BF16))
            s_scr[hd] = s_t * decay[:, kl] + _dot_tn(v_h, kd[:, kl].astype(BF16))
            o = o * lax.rsqrt(jnp.mean(o * o, axis=-1, keepdims=True) + RMS_EPS) * gn_ref[:, vl]
            o_ref[rows, vl] = (o * _silu(r[:, vl])).astype(o_ref.dtype)

    @pl.when(t == pl.num_programs(1) - 1)
    def _():
        sfin_ref[...] = s_scr[...]


def _gla(h16, wz, wa1, wa2, ba, gn, s0_t):
    bsz, seq, d = h16.shape
    _, heads, dv, dk = s0_t.shape
    chunk = GLA_CHUNK if seq % GLA_CHUNK == 0 else seq
    tc = _row_tile(seq, 512)
    kern = functools.partial(_gla_kernel, chunk=chunk, n_chunk=tc // chunk, dk=dk, dv=dv)
    state = pl.BlockSpec((None, heads, dv, dk), lambda b, t: (b, 0, 0, 0))
    return pl.pallas_call(
        kern,
        grid=(bsz, seq // tc),
        in_specs=[pl.BlockSpec((None, tc, d), lambda b, t: (b, t, 0)),
                  _const_spec(wz.shape), _const_spec(wa1.shape), _const_spec(wa2.shape),
                  _const_spec(ba.shape), _const_spec(gn.shape), state],
        out_specs=[pl.BlockSpec((None, tc, heads * dv), lambda b, t: (b, t, 0)), state],
        out_shape=[jax.ShapeDtypeStruct((bsz, seq, heads * dv), BF16),
                   jax.ShapeDtypeStruct(s0_t.shape, F32)],
        scratch_shapes=[pltpu.VMEM((heads, dv, dk), F32)],
        compiler_params=_params("parallel", "arbitrary"),
        name="gla",
    )(h16, wz, wa1, wa2, ba, gn, s0_t)


def _gmlp_kernel(h_ref, w_ref, g_ref, b_ref, ws_ref, bs_ref, o_ref, *v_ref, chunk, n_chunk, width):
    x = h_ref[...]
    z = _dot(x, w_ref[...])
    u = jax.nn.gelu(z[:, :width])
    v = _layer_norm(jax.nn.gelu(z[:, width:]), g_ref[...], b_ref[...])
    if v_ref:
        v_ref[0][...] = v
    gdim = width // GMLP_GROUPS
    tri = _lower_tri(chunk)
    for g in range(GMLP_GROUPS):
        w_g = jnp.where(tri, ws_ref[g], 0.0).astype(BF16)
        bias = bs_ref[:, g:g + 1]
        cols = slice(g * gdim, (g + 1) * gdim)
        for n in range(n_chunk):
            rows = slice(n * chunk, (n + 1) * chunk)
            mixed = _dot(w_g, v[rows, cols].astype(BF16)) + bias
            o_ref[rows, cols] = (u[rows, cols] * mixed).astype(o_ref.dtype)


def _gmlp(h16, w, g, b, ws, bs_t, keep_v):
    bsz, seq, d = h16.shape
    width = g.shape[-1]
    chunk = ws.shape[-1]
    tc = _row_tile(seq, 512)
    kern = functools.partial(_gmlp_kernel, chunk=chunk, n_chunk=tc // chunk, width=width)
    tile = pl.BlockSpec((None, tc, width), lambda i, t: (i, t, 0))
    out_specs = [tile]
    out_shape = [jax.ShapeDtypeStruct((bsz, seq, width), BF16)]
    if keep_v:
        out_specs.append(tile)
        out_shape.append(jax.ShapeDtypeStruct((bsz, seq, width), F32))
    return pl.pallas_call(
        kern,
        grid=(bsz, seq // tc),
        in_specs=[pl.BlockSpec((None, tc, d), lambda i, t: (i, t, 0)),
                  _const_spec(w.shape), _const_spec(g.shape), _const_spec(b.shape),
                  _const_spec(ws.shape), _const_spec(bs_t.shape)],
        out_specs=out_specs,
        out_shape=out_shape,
        compiler_params=_params("parallel", "parallel"),
        name="gmlp",
    )(h16, w, g, b, ws, bs_t)


def _transpose_bf16(x):
    r, c = x.shape
    if r % LANES == 0 and c % LANES == 0:
        return x.T
    row = lax.broadcasted_iota(jnp.int32, (r, r), 0)
    col = lax.broadcasted_iota(jnp.int32, (r, r), 1)
    return _dot_tn(x.astype(BF16), jnp.where(row == col, 1.0, 0.0).astype(BF16))


def _fox_pack(dcum, zq, zk, zv, q_ref, k_ref, v_ref, *, dh):
    tm = dcum.shape[0]
    hi = dcum.astype(BF16).astype(F32)
    r1 = dcum - hi
    mid = r1.astype(BF16).astype(F32)
    lo = (r1 - mid).astype(BF16).astype(F32)
    lane = lax.broadcasted_iota(jnp.int32, (tm, LANES), 1)
    sub = lax.broadcasted_iota(jnp.int32, (8, tm), 0)
    pad = jnp.zeros((LANES - dh - 8, tm), F32)
    ones_row = jnp.where(sub == 0, 1.0, 0.0)
    zv_t = _transpose_bf16(zv)
    if q_ref is not None:
        zq_t = _transpose_bf16(zq * (dh ** -0.5))
        hi_t, mid_t, lo_t = _transpose_bf16(hi), _transpose_bf16(mid), _transpose_bf16(lo)
    for h in range(FOX_HEADS):
        cols = slice((h // 2) * LANES, (h // 2 + 1) * LANES)
        odd = h % 2
        data = (lane >= dh) if odd else (lane < dh)
        slot = lane - (0 if odd else dh)
        d_hi = jnp.broadcast_to(hi[:, h:h + 1], (tm, LANES))
        d_mid = jnp.broadcast_to(mid[:, h:h + 1], (tm, LANES))
        d_lo = jnp.broadcast_to(lo[:, h:h + 1], (tm, LANES))
        ones_first = jnp.where(slot < 0, 0.0, jnp.where(slot < 3, 1.0, 0.0))
        k_extra = jnp.where(slot == 3, -d_hi, jnp.where(slot == 4, -d_mid, jnp.where(slot == 5, -d_lo, ones_first)))
        k_ref[h] = jnp.where(data, zk[:, cols], k_extra).astype(k_ref.dtype)
        rows = slice(h * dh, (h + 1) * dh)
        v_ref[h] = jnp.concatenate([zv_t[rows], ones_row, pad], axis=0).astype(v_ref.dtype)
        if q_ref is not None:
            bias = jnp.where(sub == 0, jnp.broadcast_to(hi_t[h:h + 1], (8, tm)),
                             jnp.where(sub == 1, jnp.broadcast_to(mid_t[h:h + 1], (8, tm)),
                                       jnp.where(sub == 2, jnp.broadcast_to(lo_t[h:h + 1], (8, tm)),
                                                 jnp.where(sub < 6, 1.0, 0.0))))
            parts = [bias, pad, zq_t[rows]] if odd else [zq_t[rows], bias, pad]
            q_ref[h] = jnp.concatenate(parts, axis=0).astype(q_ref.dtype)


def _fox_proj_kernel(h_ref, w_ref, wf_ref, bf_ref, d0_ref, kout_ref, vout_ref, lf_ref,
                     q_ref, k_ref, v_ref, d_scr, *, width):
    t = pl.program_id(1)

    @pl.when(t == 0)
    def _():
        d_scr[...] = d0_ref[...]

    tm = h_ref.shape[0]
    x = h_ref[...]
    z = _dot(x, w_ref[...])
    zq, zk, zv = z[:, :width], z[:, width:2 * width], z[:, 2 * width:]
    kout_ref[...] = zk
    vout_ref[...] = zv
    lf = _log_sigmoid(_dot(x, wf_ref[...]) + bf_ref[...])
    lane = lax.broadcasted_iota(jnp.int32, lf.shape, 1)
    lf = jnp.where(lane < FOX_HEADS, lf, 0.0)
    lf_ref[...] = lf[:, :FOX_HEADS]
    tri_bf = jnp.where(_lower_tri(tm), 1.0, 0.0).astype(BF16)
    dcum = _tri_cumsum(tri_bf, lf) + d_scr[...]
    d_scr[...] = dcum[tm - 1:tm, :]
    _fox_pack(dcum, zq, zk, zv, q_ref, k_ref, v_ref, dh=width // FOX_HEADS)


def _fox_proj(h16, w, wf, bf, d0):
    bsz, seq, d = h16.shape
    width = w.shape[1] // 3
    tm = _row_tile(seq, 512)
    kern = functools.partial(_fox_proj_kernel, width=width)
    tile = pl.BlockSpec((None, tm, width), lambda b, t: (b, t, 0))
    rows = pl.BlockSpec((None, FOX_HEADS, tm, LANES), lambda b, t: (b, 0, t, 0))
    cols = pl.BlockSpec((None, FOX_HEADS, LANES, tm), lambda b, t: (b, 0, 0, t))
    rows_shape = jax.ShapeDtypeStruct((bsz, FOX_HEADS, seq, LANES), BF16)
    cols_shape = jax.ShapeDtypeStruct((bsz, FOX_HEADS, LANES, seq), BF16)
    return pl.pallas_call(
        kern,
        grid=(bsz, seq // tm),
        in_specs=[pl.BlockSpec((None, tm, d), lambda b, t: (b, t, 0)),
                  _const_spec(w.shape), _const_spec(wf.shape), _const_spec(bf.shape),
                  pl.BlockSpec((None, 1, LANES), lambda b, t: (b, 0, 0))],
        out_specs=[tile, tile, pl.BlockSpec((None, tm, FOX_HEADS), lambda b, t: (b, t, 0)),
                   cols, rows, cols],
        out_shape=[jax.ShapeDtypeStruct((bsz, seq, width), F32), jax.ShapeDtypeStruct((bsz, seq, width), F32),
                   jax.ShapeDtypeStruct((bsz, seq, FOX_HEADS), F32),
                   cols_shape, rows_shape, cols_shape],
        scratch_shapes=[pltpu.VMEM((1, LANES), F32)],
        compiler_params=_params("parallel", "arbitrary"),
        name="fox_proj",
    )(h16, w, wf, bf, d0)


def _fox_cache_kernel(ck_ref, cv_ref, cl_ref, k_ref, v_ref, dend_ref, d_scr, *, width):
    t = pl.program_id(1)

    @pl.when(t == 0)
    def _():
        d_scr[...] = jnp.zeros_like(d_scr)

    tm = ck_ref.shape[0]
    tri_bf = jnp.where(_lower_tri(tm), 1.0, 0.0).astype(BF16)
    dcum = _tri_cumsum(tri_bf, cl_ref[...]) + d_scr[...]
    d_scr[...] = dcum[tm - 1:tm, :]
    _fox_pack(dcum, None, ck_ref[...], cv_ref[...], None, k_ref, v_ref, dh=width // FOX_HEADS)

    @pl.when(t == pl.num_programs(1) - 1)
    def _():
        lane = lax.broadcasted_iota(jnp.int32, (1, LANES), 1)
        row = jnp.zeros((1, LANES), F32)
        for h in range(FOX_HEADS):
            row = jnp.where(lane == h, dcum[tm - 1:tm, h:h + 1], row)
        dend_ref[...] = row


def _fox_cache(ck, cv, cl):
    bsz, past, width = ck.shape
    tm = _row_tile(past, 512)
    kern = functools.partial(_fox_cache_kernel, width=width)
    tile = pl.BlockSpec((None, tm, width), lambda b, t: (b, t, 0))
    return pl.pallas_call(
        kern,
        grid=(bsz, past // tm),
        in_specs=[tile, tile, pl.BlockSpec((None, tm, FOX_HEADS), lambda b, t: (b, t, 0))],
        out_specs=[pl.BlockSpec((None, FOX_HEADS, tm, LANES), lambda b, t: (b, 0, t, 0)),
                   pl.BlockSpec((None, FOX_HEADS, LANES, tm), lambda b, t: (b, 0, 0, t)),
                   pl.BlockSpec((None, 1, LANES), lambda b, t: (b, 0, 0))],
        out_shape=[jax.ShapeDtypeStruct((bsz, FOX_HEADS, past, LANES), BF16),
                   jax.ShapeDtypeStruct((bsz, FOX_HEADS, LANES, past), BF16),
                   jax.ShapeDtypeStruct((bsz, 1, LANES), F32)],
        scratch_shapes=[pltpu.VMEM((1, FOX_HEADS), F32)],
        compiler_params=_params("parallel", "arbitrary"),
        name="fox_cache",
    )(ck, cv, cl)


def _fox_attn_kernel(q_ref, kp_ref, vp_ref, kn_ref, vn_ref, o_ref, s_scr, sd_scr, acc_scr,
                     *, tk, dh, full_base, full_per_q):
    tq = q_ref.shape[2]
    n_full = full_base + pl.program_id(2) * full_per_q
    q = [q_ref[hh] for hh in range(2)]
    causal = lax.broadcasted_iota(jnp.int32, (tq, tq), 0) <= lax.broadcasted_iota(jnp.int32, (tq, tq), 1)

    def scores(width):
        def step(i, ms):
            rows = pl.ds(pl.multiple_of(i * width, width), width)
            out = []
            for hh in range(2):
                s = _dot(kp_ref[hh, rows, :], q[hh])
                s_scr[hh, rows, :] = s
                out.append(jnp.maximum(ms[hh], jnp.max(s, axis=0, keepdims=True)))
            return tuple(out)
        return step

    def accumulate(width):
        def step(i, carry):
            rows = pl.ds(pl.multiple_of(i * width, width), width)
            for hh in range(2):
                p = jnp.exp(s_scr[hh, rows, :] - ms[hh]).astype(BF16)
                acc_scr[hh] += _dot(vp_ref[hh, :, rows], p)
            return carry
        return step

    n_group = n_full // ATTN_GROUP
    ms = tuple(jnp.full((1, tq), -jnp.inf, F32) for _ in range(2))
    ms = lax.fori_loop(0, n_group, scores(ATTN_GROUP * tk), ms)
    ms = list(lax.fori_loop(n_group * ATTN_GROUP, n_full, scores(tk), ms))
    for hh in range(2):
        s = jnp.where(causal, _dot(kn_ref[hh], q[hh]), -jnp.inf)
        sd_scr[hh] = s
        ms[hh] = jnp.maximum(ms[hh], jnp.max(s, axis=0, keepdims=True))
        acc_scr[hh] = _dot(vn_ref[hh], jnp.exp(sd_scr[hh] - ms[hh]).astype(BF16))
    lax.fori_loop(0, n_group, accumulate(ATTN_GROUP * tk), 0)
    lax.fori_loop(n_group * ATTN_GROUP, n_full, accumulate(tk), 0)
    halves = []
    for hh in range(2):
        acc = acc_scr[hh]
        halves.append(acc[:dh] / acc[dh:dh + 1])
    o_ref[...] = _transpose_bf16(jnp.concatenate(halves, axis=0)).astype(o_ref.dtype)


def _fox_attn(q_t, kp, vp_t, kn, vn_t, *, dh):
    bsz, heads, _, seq = q_t.shape
    self_attn = kp is kn
    past = kp.shape[2]
    off = 0 if self_attn else past
    tq = _row_tile(seq, 2 * LANES)
    nq = seq // tq
    tk = tq if self_attn else _row_tile(past, 512)
    assert off % tk == 0 and (nq == 1 or tq % tk == 0), (off, tq, tk)
    full_base, full_per_q = off // tk, tq // tk
    max_full = max(full_base + (nq - 1) * full_per_q, 1)
    kern = functools.partial(_fox_attn_kernel, tk=tk, dh=dh, full_base=full_base, full_per_q=full_per_q)
    return pl.pallas_call(
        kern,
        grid=(bsz, heads // 2, nq),
        in_specs=[pl.BlockSpec((None, 2, LANES, tq), lambda b, j, i: (b, j, 0, i)),
                  pl.BlockSpec((None, 2, past, LANES), lambda b, j, i: (b, j, 0, 0)),
                  pl.BlockSpec((None, 2, LANES, past), lambda b, j, i: (b, j, 0, 0)),
                  pl.BlockSpec((None, 2, tq, LANES), lambda b, j, i: (b, j, i, 0)),
                  pl.BlockSpec((None, 2, LANES, tq), lambda b, j, i: (b, j, 0, i))],
        out_specs=pl.BlockSpec((None, tq, LANES), lambda b, j, i: (b, i, j)),
        out_shape=jax.ShapeDtypeStruct((bsz, seq, heads // 2 * LANES), BF16),
        scratch_shapes=[pltpu.VMEM((2, max_full * tk, tq), F32), pltpu.VMEM((2, tq, tq), F32),
                        pltpu.VMEM((2, LANES, tq), F32)],
        compiler_params=_params("parallel", "parallel", "arbitrary"),
        name="fox_attn",
    )(q_t, kp, vp_t, kn, vn_t)


def _merge_kernel(h32_ref, h16_ref, og_ref, om_ref, of_ref, wg_ref, wb_ref, wo_ref, lg_ref, lb_ref,
                  o32_ref, o16_ref, *, alpha):
    d = h32_ref.shape[1]
    x = h16_ref[...]
    m = None
    for r, o_ref in enumerate((og_ref, om_ref, of_ref)):
        gate = jax.nn.sigmoid(_dot(x, wg_ref[:, r * d:(r + 1) * d]))
        term = gate * _dot(o_ref[...], wb_ref[r])
        m = term if m is None else m + term
    y = alpha * h32_ref[...] + _dot(m.astype(BF16), wo_ref[...])
    y = _layer_norm(y, lg_ref[...], lb_ref[...])
    o32_ref[...] = y
    o16_ref[...] = y.astype(BF16)


def _merge(h32, h16, o_gla, o_gm, o_fox, wg, wb, wo, lg, lb, alpha):
    n, d = h32.shape
    w = o_gla.shape[1]
    tm = _row_tile(n, 256)
    row = pl.BlockSpec((tm, d), lambda i: (i, 0))
    mix = pl.BlockSpec((tm, w), lambda i: (i, 0))
    return pl.pallas_call(
        functools.partial(_merge_kernel, alpha=alpha),
        grid=(n // tm,),
        in_specs=[row, row, mix, mix, mix, _const_spec(wg.shape), _const_spec(wb.shape),
                  _const_spec(wo.shape), _const_spec(lg.shape), _const_spec(lb.shape)],
        out_specs=[row, row],
        out_shape=[jax.ShapeDtypeStruct((n, d), F32), jax.ShapeDtypeStruct((n, d), BF16)],
        compiler_params=_params("parallel"),
        name="merge",
    )(h32, h16, o_gla, o_gm, o_fox, wg, wb, wo, lg, lb)


def _ffn_kernel(h32_ref, h16_ref, wg_ref, wu_ref, wd_ref, lg_ref, lb_ref, o32_ref, o16_ref, acc_scr, *, alpha):
    j = pl.program_id(1)

    @pl.when(j == 0)
    def _():
        acc_scr[...] = jnp.zeros_like(acc_scr)

    x = h16_ref[...]
    a = _silu(_dot(x, wg_ref[...])) * _dot(x, wu_ref[...])
    acc_scr[...] += _dot(a.astype(BF16), wd_ref[...])

    @pl.when(j == pl.num_programs(1) - 1)
    def _():
        y = _layer_norm(alpha * h32_ref[...] + acc_scr[...], lg_ref[...], lb_ref[...])
        o32_ref[...] = y
        o16_ref[...] = y.astype(BF16)


def _ffn(h32, h16, wg, wu, wd, lg, lb, alpha):
    n, d = h32.shape
    f = wg.shape[1]
    tm = _row_tile(n, 1024)
    tf = _row_tile(f, 512)
    row = pl.BlockSpec((tm, d), lambda i, j: (i, 0))
    return pl.pallas_call(
        functools.partial(_ffn_kernel, alpha=alpha),
        grid=(n // tm, f // tf),
        in_specs=[row, row, pl.BlockSpec((d, tf), lambda i, j: (0, j)), pl.BlockSpec((d, tf), lambda i, j: (0, j)),
                  pl.BlockSpec((tf, d), lambda i, j: (j, 0)), _const_spec(lg.shape), _const_spec(lb.shape)],
        out_specs=[row, row],
        out_shape=[jax.ShapeDtypeStruct((n, d), F32), jax.ShapeDtypeStruct((n, d), BF16)],
        scratch_shapes=[pltpu.VMEM((tm, d), F32)],
        compiler_params=_params("parallel", "arbitrary"),
        name="ffn",
    )(h32, h16, wg, wu, wd, lg, lb)


R_W1, R_W2, R_E1, R_E2, R_RANK1, R_RANK2 = range(6)


def _route_kernel(h_ref, w_ref, r_ref, cnt_ref, cnt_scr, *, n_experts):
    @pl.when(pl.program_id(0) == 0)
    def _():
        cnt_scr[...] = jnp.zeros_like(cnt_scr)

    tm = h_ref.shape[0]
    logits = jnp.dot(h_ref[...], w_ref[...], preferred_element_type=F32, precision=lax.Precision.HIGHEST)
    lane = lax.broadcasted_iota(jnp.int32, logits.shape, 1)
    logits = jnp.where(lane < n_experts, logits, -jnp.inf)
    v1 = jnp.max(logits, axis=-1, keepdims=True)
    i1 = jnp.min(jnp.where(logits == v1, lane, LANES), axis=-1, keepdims=True)
    rest = jnp.where(lane == i1, -jnp.inf, logits)
    v2 = jnp.max(rest, axis=-1, keepdims=True)
    i2 = jnp.min(jnp.where(rest == v2, lane, LANES), axis=-1, keepdims=True)
    e2 = jnp.exp(v2 - v1)
    w1 = 1.0 / (1.0 + e2)
    w2 = e2 / (1.0 + e2)
    hot1 = lane == i1
    hot2 = lane == i2
    sel = jnp.where(hot1, 1.0, jnp.where(hot2, 1.0, 0.0))
    tri_bf = jnp.where(_lower_tri(tm), 1.0, 0.0).astype(BF16)
    incl = _dot(tri_bf, sel.astype(BF16)) + cnt_scr[...]
    excl = incl - sel
    rank1 = jnp.sum(jnp.where(hot1, excl, 0.0), axis=-1, keepdims=True)
    rank2 = jnp.sum(jnp.where(hot2, excl, 0.0), axis=-1, keepdims=True)
    cnt_scr[...] = incl[tm - 1:tm, :]
    cnt_ref[...] = incl[tm - 1:tm, :]
    rec = jnp.zeros(logits.shape, F32)
    for slot, val in ((R_W1, w1), (R_W2, w2), (R_E1, i1.astype(F32)), (R_E2, i2.astype(F32)),
                      (R_RANK1, rank1), (R_RANK2, rank2)):
        rec = jnp.where(lane == slot, val, rec)
    r_ref[...] = rec


def _route(h32, w_router):
    n, d = h32.shape
    n_experts = w_router.shape[1]
    w_pad = jnp.pad(w_router, ((0, 0), (0, LANES - n_experts)))
    tm = _row_tile(n, 512)
    return pl.pallas_call(
        functools.partial(_route_kernel, n_experts=n_experts),
        grid=(n // tm,),
        in_specs=[pl.BlockSpec((tm, d), lambda i: (i, 0)), _const_spec(w_pad.shape)],
        out_specs=[pl.BlockSpec((tm, LANES), lambda i: (i, 0)), _const_spec((1, LANES))],
        out_shape=[jax.ShapeDtypeStruct((n, LANES), F32), jax.ShapeDtypeStruct((1, LANES), F32)],
        scratch_shapes=[pltpu.VMEM((1, LANES), F32)],
        compiler_params=_params("arbitrary"),
        name="route",
    )(h32, w_pad)


def _dispatch_kernel(pos_ref, h_ref, xin_ref, xg_ref, sem):
    del xin_ref
    tm = pos_ref.shape[1]
    base = pl.program_id(0) * tm

    def row_copy(r, s):
        return pltpu.make_async_copy(h_ref.at[pl.ds(base + r, 1)], xg_ref.at[pl.ds(pos_ref[s, r], 1)], sem)

    def start(r, carry):
        for s in range(TOP_K):
            row_copy(r, s).start()
        return carry

    def wait(r, carry):
        for s in range(TOP_K):
            row_copy(r, s).wait()
        return carry

    lax.fori_loop(0, tm, start, 0)
    lax.fori_loop(0, tm, wait, 0)


def _dispatch(pos, h32, n_rows):
    n, d = h32.shape
    tm = _row_tile(n, 1024)
    return pl.pallas_call(
        _dispatch_kernel,
        grid=(n // tm,),
        in_specs=[pl.BlockSpec((TOP_K, tm), lambda i: (0, i), memory_space=pltpu.SMEM),
                  pl.BlockSpec(memory_space=pl.ANY), pl.BlockSpec(memory_space=pl.ANY)],
        out_specs=pl.BlockSpec(memory_space=pl.ANY),
        out_shape=jax.ShapeDtypeStruct((n_rows, d), F32),
        scratch_shapes=[pltpu.SemaphoreType.DMA(())],
        input_output_aliases={2: 0},
        compiler_params=_params("arbitrary"),
        name="moe_dispatch",
    )(pos, h32, jnp.zeros((n_rows, d), F32))


def _group_ffn_kernel(te_ref, nt_ref, x_ref, wg_ref, wu_ref, wd_ref, y_ref, acc_scr, x16_scr):
    del te_ref
    j = pl.program_id(1)
    last = pl.num_programs(1) - 1
    live = pl.program_id(0) < nt_ref[0]

    @pl.when(live & (j == 0))
    def _():
        x16_scr[...] = x_ref[...].astype(BF16)
        acc_scr[...] = jnp.zeros_like(acc_scr)

    @pl.when(live)
    def _():
        x = x16_scr[...]
        a = _silu(_dot(x, wg_ref[...])) * _dot(x, wu_ref[...])
        acc_scr[...] += _dot(a.astype(BF16), wd_ref[...])

    @pl.when(live & (j == last))
    def _():
        y_ref[...] = acc_scr[...]

    @pl.when(jnp.logical_not(live) & (j == last))
    def _():
        y_ref[...] = jnp.zeros_like(y_ref)


def _group_ffn(tile_expert, n_tiles, xg, wg, wu, wd, tm):
    n_rows, d = xg.shape
    f = wg.shape[2]
    tf = _row_tile(f, 512)
    nf = f // tf

    def col(i, j, te, nt):
        return jnp.where(i < nt[0], j, nf - 1)

    grid_spec = pltpu.PrefetchScalarGridSpec(
        num_scalar_prefetch=2,
        grid=(n_rows // tm, nf),
        in_specs=[pl.BlockSpec((tm, d), lambda i, j, te, nt: (i, 0)),
                  pl.BlockSpec((None, d, tf), lambda i, j, te, nt: (te[i], 0, col(i, j, te, nt))),
                  pl.BlockSpec((None, d, tf), lambda i, j, te, nt: (te[i], 0, col(i, j, te, nt))),
                  pl.BlockSpec((None, tf, d), lambda i, j, te, nt: (te[i], col(i, j, te, nt), 0))],
        out_specs=pl.BlockSpec((tm, d), lambda i, j, te, nt: (i, 0)),
        scratch_shapes=[pltpu.VMEM((tm, d), F32), pltpu.VMEM((tm, d), BF16)],
    )
    return pl.pallas_call(
        _group_ffn_kernel,
        grid_spec=grid_spec,
        out_shape=jax.ShapeDtypeStruct((n_rows, d), F32),
        compiler_params=_params("arbitrary", "arbitrary"),
        name="moe_ffn",
    )(tile_expert, n_tiles, xg, wg, wu, wd)


def _combine_kernel(pos_ref, r_ref, h32_ref, y_ref, lg_ref, lb_ref, o32_ref, o16_ref, g_scr, sem, *, alpha):
    tm = h32_ref.shape[0]

    def row_copy(r, s):
        return pltpu.make_async_copy(y_ref.at[pl.ds(pos_ref[s, r], 1)], g_scr.at[s, pl.ds(r, 1)], sem)

    def start(r, carry):
        for s in range(TOP_K):
            row_copy(r, s).start()
        return carry

    def wait(r, carry):
        for s in range(TOP_K):
            row_copy(r, s).wait()
        return carry

    lax.fori_loop(0, tm, start, 0)
    lax.fori_loop(0, tm, wait, 0)
    rec = r_ref[...]
    f = rec[:, R_W1:R_W1 + 1] * g_scr[0] + rec[:, R_W2:R_W2 + 1] * g_scr[1]
    y = _layer_norm(alpha * h32_ref[...] + f, lg_ref[...], lb_ref[...])
    o32_ref[...] = y
    o16_ref[...] = y.astype(BF16)


def _combine(pos, rec, h32, yg, lg, lb, alpha):
    n, d = h32.shape
    tm = _row_tile(n, 256)
    row = pl.BlockSpec((tm, d), lambda i: (i, 0))
    return pl.pallas_call(
        functools.partial(_combine_kernel, alpha=alpha),
        grid=(n // tm,),
        in_specs=[pl.BlockSpec((TOP_K, tm), lambda i: (0, i), memory_space=pltpu.SMEM),
                  pl.BlockSpec((tm, LANES), lambda i: (i, 0)), row,
                  pl.BlockSpec(memory_space=pl.ANY), _const_spec(lg.shape), _const_spec(lb.shape)],
        out_specs=[row, row],
        out_shape=[jax.ShapeDtypeStruct((n, d), F32), jax.ShapeDtypeStruct((n, d), BF16)],
        scratch_shapes=[pltpu.VMEM((TOP_K, tm, d), F32), pltpu.SemaphoreType.DMA(())],
        compiler_params=_params("arbitrary"),
        name="moe_combine",
    )(pos, rec, h32, yg, lg, lb)


def _moe(h32, w_router, wg, wu, wd, lg, lb, alpha):
    n, d = h32.shape
    n_experts = wg.shape[0]
    tm = 1024 if n >= 8192 else 256
    max_tiles = (TOP_K * n) // tm + n_experts
    rec, counts = _route(h32, w_router)

    cnt = counts[0, :n_experts].astype(jnp.int32)
    ends = jnp.cumsum((cnt + tm - 1) // tm * tm)
    starts = ends - (cnt + tm - 1) // tm * tm
    expert = rec[:, R_E1:R_E2 + 1].astype(jnp.int32)
    rank = rec[:, R_RANK1:R_RANK2 + 1].astype(jnp.int32)
    pos = (starts[expert] + rank).T
    n_tiles = (ends[-1] // tm).astype(jnp.int32)
    tile_start = jnp.arange(max_tiles, dtype=jnp.int32) * tm
    tile_expert = jnp.minimum(jnp.searchsorted(ends, tile_start, side='right'), n_experts - 1).astype(jnp.int32)
    tile_expert = jnp.where(jnp.arange(max_tiles) < n_tiles, tile_expert, tile_expert[jnp.maximum(n_tiles - 1, 0)])

    xg = _dispatch(pos, h32, max_tiles * tm)
    yg = _group_ffn(tile_expert, n_tiles.reshape(1), xg, wg, wu, wd, tm)
    return _combine(pos, rec, h32, yg, lg, lb, alpha)


def _layer_weights(w, l, d, mix_w):
    rank = w['w_a2'].shape[1]
    qk_w = w['w_a2'].shape[2]
    sizes = (qk_w, qk_w, mix_w, mix_w, rank, mix_w, mix_w, mix_w, FOX_HEADS, mix_w, mix_w, 3 * d)
    offs = [0]
    for s in sizes:
        offs.append(offs[-1] + s)
    w_in = w['w_in'][l]
    col = lambda a, b: w_in[:, offs[a]:offs[b]]
    pad_cols = lambda x: jnp.pad(x, ((0, 0), (0, LANES - x.shape[1])))
    return dict(
        gla_wz=col(0, 4).astype(BF16),
        gla_wa1=pad_cols(col(4, 5)).astype(BF16),
        gla_wa2=jnp.pad(w['w_a2'][l], ((0, LANES - rank), (0, 0))).astype(BF16),
        gla_ba=w['b_a'][l].reshape(1, -1),
        gla_gn=w['gla_norm_g'][l].reshape(1, -1),
        fox_w=col(5, 8).astype(BF16),
        fox_wf=pad_cols(col(8, 9)).astype(BF16),
        fox_bf=pad_cols(w['b_f'][l].reshape(1, -1)),
        gm_w=col(9, 11).astype(BF16),
        gm_g=w['gmlp_norm_g'][l].reshape(1, -1),
        gm_b=w['gmlp_norm_b'][l].reshape(1, -1),
        gate_w=col(11, 12).astype(BF16),
        branch_w=w['w_branch'][l].astype(BF16),
        out_w=w['w_out'][l].astype(BF16),
    )


def _trunk(x, w, gla_s, fox_k, fox_v, fox_lf, keep_gmlp_rows):
    bsz, seq, d = x.shape
    depth = w['w_in'].shape[0]
    mix_w = w['gla_norm_g'].shape[1]
    alpha = (2 * depth) ** 0.25
    n = bsz * seq
    dk = w['w_a2'].shape[2] // GLA_HEADS
    dv = mix_w // GLA_HEADS
    dh = mix_w // FOX_HEADS
    gm_chunk = GMLP_CHUNK if seq % GMLP_CHUNK == 0 else seq

    h32, h16 = _ln_in(x.reshape(n, d), w['ln_in_g'], w['ln_in_b'])
    s_out, k_out, v_out, lf_out, gm_out = [], [], [], [], []
    for l in range(depth):
        lw = _layer_weights(w, l, d, mix_w)
        h16_seq = h16.reshape(bsz, seq, d)

        if gla_s is None:
            s0_t = jnp.zeros((bsz, GLA_HEADS, dv, dk), F32)
        else:
            s0_t = jnp.swapaxes(gla_s[l], -1, -2)
        o_gla, s_t = _gla(h16_seq, lw['gla_wz'], lw['gla_wa1'], lw['gla_wa2'], lw['gla_ba'], lw['gla_gn'], s0_t)
        s_out.append(jnp.swapaxes(s_t, -1, -2))

        gm = _gmlp(h16_seq, lw['gm_w'], lw['gm_g'], lw['gm_b'], w['gmlp_ws'][l][:, :gm_chunk, :gm_chunk],
                   w['gmlp_bs'][l][:, :gm_chunk].T, keep_gmlp_rows)
        o_gm = gm[0]
        if keep_gmlp_rows:
            gm_out.append(gm[1])

        if fox_k is None:
            d0 = jnp.zeros((bsz, 1, LANES), F32)
        else:
            past = fox_k.shape[2]
            kp, vp_t, d0 = _fox_cache(fox_k[l].reshape(bsz, past, mix_w), fox_v[l].reshape(bsz, past, mix_w),
                                      fox_lf[l])
        k_new, v_new, lf_new, q_t, kn, vn_t = _fox_proj(h16_seq, lw['fox_w'], lw['fox_wf'], lw['fox_bf'], d0)
        if fox_k is None:
            kp, vp_t = kn, vn_t
        o_fox = _fox_attn(q_t, kp, vp_t, kn, vn_t, dh=dh)
        k_out.append(k_new.reshape(bsz, seq, FOX_HEADS, dh))
        v_out.append(v_new.reshape(bsz, seq, FOX_HEADS, dh))
        lf_out.append(lf_new)

        h32, h16 = _merge(h32, h16, o_gla.reshape(n, mix_w), o_gm.reshape(n, mix_w), o_fox.reshape(n, mix_w),
                          lw['gate_w'], lw['branch_w'], lw['out_w'],
                          w['ln_g'][l, 0].reshape(1, d), w['ln_b'][l, 0].reshape(1, d), alpha)
        lg, lb = w['ln_g'][l, 1].reshape(1, d), w['ln_b'][l, 1].reshape(1, d)
        j = l // 2
        if l % 2 == 0:
            h32, h16 = _ffn(h32, h16, w['ffn_w_gate'][j].astype(BF16), w['ffn_w_up'][j].astype(BF16),
                            w['ffn_w_down'][j].astype(BF16), lg, lb, alpha)
        else:
            h32, h16 = _moe(h32, w['moe_router'][j], w['moe_w_gate'][j].astype(BF16),
                            w['moe_w_up'][j].astype(BF16), w['moe_w_down'][j].astype(BF16), lg, lb, alpha)
    gm_stack = jnp.stack(gm_out) if keep_gmlp_rows else None
    return (h32.reshape(bsz, seq, d), jnp.stack(s_out), jnp.stack(k_out), jnp.stack(v_out), jnp.stack(lf_out),
            gm_stack)


def kernel(x_prompt, x_sample, state_gla, cache_fox_k, cache_fox_v, cache_fox_logf, ln_in_g, ln_in_b, w_in, w_a2, b_a, gla_norm_g, b_f, gmlp_norm_g, gmlp_norm_b, gmlp_ws, gmlp_bs, w_branch, w_out, ln_g, ln_b, ffn_w_gate, ffn_w_up, ffn_w_down, moe_router, moe_w_gate, moe_w_up, moe_w_down):
    w = dict(ln_in_g=ln_in_g, ln_in_b=ln_in_b, w_in=w_in, w_a2=w_a2, b_a=b_a, gla_norm_g=gla_norm_g,
             b_f=b_f, gmlp_norm_g=gmlp_norm_g, gmlp_norm_b=gmlp_norm_b, gmlp_ws=gmlp_ws, gmlp_bs=gmlp_bs,
             w_branch=w_branch, w_out=w_out, ln_g=ln_g, ln_b=ln_b, ffn_w_gate=ffn_w_gate,
             ffn_w_up=ffn_w_up, ffn_w_down=ffn_w_down, moe_router=moe_router, moe_w_gate=moe_w_gate,
             moe_w_up=moe_w_up, moe_w_down=moe_w_down)
    y_p, gla_p, fk_p, fv_p, flf_p, _ = _trunk(x_prompt, w, None, None, None, None, False)
    y_s, gla_s, fk_s, fv_s, flf_s, gmv_s = _trunk(x_sample, w, state_gla, cache_fox_k, cache_fox_v,
                                                  cache_fox_logf, True)
    return (y_p, y_s, gla_p, fk_p, fv_p, flf_p, gla_s, fk_s, fv_s, flf_s, gmv_s)
```

```python
import functools

import jax
import jax.numpy as jnp
from jax import lax
from jax.experimental import pallas as pl
from jax.experimental.pallas import tpu as pltpu

F32 = jnp.float32
BF16 = jnp.bfloat16

GLA_HEADS = 4
GLA_CHUNK = 64
GLA_TAU = 16.0
FOX_HEADS = 8
GMLP_GROUPS = 4
GMLP_CHUNK = 128
TOP_K = 2
ATTN_GROUPS = (8, 4, 2, 1)
LN_EPS = 1e-5
RMS_EPS = 1e-6

V7X_VMEM_BYTES = 64 * 1024 * 1024
VMEM_LIMIT_BYTES = V7X_VMEM_BYTES * 3 // 4
LANES = 128


def _params(*semantics, flags=None):
    return pltpu.CompilerParams(dimension_semantics=semantics, vmem_limit_bytes=VMEM_LIMIT_BYTES, flags=flags)


def _const_spec(shape):
    zeros = (0,) * len(shape)
    return pl.BlockSpec(shape, lambda *_: zeros)


def _dot(a, b):
    return jnp.dot(a, b, preferred_element_type=F32)


def _dot_nt(a, b):
    return lax.dot_general(a, b, (((1,), (1,)), ((), ())), preferred_element_type=F32)


def _dot_tn(a, b):
    return lax.dot_general(a, b, (((0,), (0,)), ((), ())), preferred_element_type=F32)


def _layer_norm(x, g, b):
    mu = jnp.mean(x, axis=-1, keepdims=True)
    xc = x - mu
    var = jnp.mean(xc * xc, axis=-1, keepdims=True)
    return xc * lax.rsqrt(var + LN_EPS) * g + b


def _log_sigmoid(x):
    return -(jnp.maximum(-x, 0.0) + jnp.log1p(jnp.exp(-jnp.abs(x))))


def _silu(x):
    return x * jax.nn.sigmoid(x)


def _split3(x):
    hi = x.astype(BF16)
    r1 = x - hi.astype(F32)
    mid = r1.astype(BF16)
    lo = (r1 - mid.astype(F32)).astype(BF16)
    return hi, mid, lo


def _tri_cumsum(tri_bf, x):
    hi, mid, lo = _split3(x)
    return _dot(tri_bf, hi) + _dot(tri_bf, mid) + _dot(tri_bf, lo)


def _lower_tri(n):
    row = lax.broadcasted_iota(jnp.int32, (n, n), 0)
    col = lax.broadcasted_iota(jnp.int32, (n, n), 1)
    return row >= col


def _row_tile(n, cap):
    t = min(n, cap)
    assert n % t == 0, (n, cap)
    return t


def _ln_in_kernel(x_ref, g_ref, b_ref, o32_ref, o16_ref):
    y = _layer_norm(x_ref[...], g_ref[...], b_ref[...])
    o32_ref[...] = y
    o16_ref[...] = y.astype(BF16)


def _ln_in(x2, g, b):
    n, d = x2.shape
    tm = _row_tile(n, 512)
    row = pl.BlockSpec((tm, d), lambda i: (i, 0))
    return pl.pallas_call(
        _ln_in_kernel,
        grid=(n // tm,),
        in_specs=[row, _const_spec((1, d)), _const_spec((1, d))],
        out_specs=[row, row],
        out_shape=[jax.ShapeDtypeStruct((n, d), F32), jax.ShapeDtypeStruct((n, d), BF16)],
        compiler_params=_params("parallel"),
        name="ln_in",
    )(x2, g.reshape(1, d), b.reshape(1, d))


def _gla_kernel(h_ref, wz_ref, wa1_ref, wa2_ref, ba_ref, gn_ref, s0_ref, o_ref, sfin_ref, s_scr,
                *, chunk, n_chunk, dk, dv):
    t = pl.program_id(1)

    @pl.when(t == 0)
    def _():
        s_scr[...] = s0_ref[...]

    qk_w = GLA_HEADS * dk
    v_w = GLA_HEADS * dv
    x = h_ref[...]
    z = _dot(x, wz_ref[...])
    a1 = _dot(x, wa1_ref[...])
    log_a = _log_sigmoid(_dot(a1.astype(BF16), wa2_ref[...]) + ba_ref[...]) * (1.0 / GLA_TAU)

    tri = _lower_tri(chunk)
    tri_bf = jnp.where(tri, 1.0, 0.0).astype(BF16)
    for c in range(n_chunk):
        rows = slice(c * chunk, (c + 1) * chunk)
        b = _tri_cumsum(tri_bf, log_a[rows])
        b_last = b[chunk - 1:chunk, :]
        q = z[rows, 0:qk_w]
        k = z[rows, qk_w:2 * qk_w]
        v = z[rows, 2 * qk_w:2 * qk_w + v_w]
        r = z[rows, 2 * qk_w + v_w:2 * qk_w + 2 * v_w]
        qe = (q * (dk ** -0.5)) * jnp.exp(b)
        ke = k * jnp.exp(-b)
        kd = k * jnp.exp(b_last - b)
        decay = jnp.exp(b_last)
        for hd in range(GLA_HEADS):
            kl = slice(hd * dk, (hd + 1) * dk)
            vl = slice(hd * dv, (hd + 1) * dv)
            qe_h = qe[:, kl].astype(BF16)
            v_h = v[:, vl].astype(BF16)
            a = jnp.where(tri, _dot_nt(qe_h, ke[:, kl].astype(BF16)), 0.0)
            s_t = s_scr[hd]
            o = _dot(a.astype(BF16), v_h) + _dot_nt(qe_h, s_t.astype(BF16))
            s_scr[hd] = s_t * decay[:, kl] + _dot_tn(v_h, kd[:, kl].astype(BF16))
            o = o * lax.rsqrt(jnp.mean(o * o, axis=-1, keepdims=True) + RMS_EPS) * gn_ref[:, vl]
            o_ref[rows, vl] = (o * _silu(r[:, vl])).astype(o_ref.dtype)

    @pl.when(t == pl.num_programs(1) - 1)
    def _():
        sfin_ref[...] = s_scr[...]


def _gla(h16, wz, wa1, wa2, ba, gn, s0_t):
    bsz, seq, d = h16.shape
    _, heads, dv, dk = s0_t.shape
    chunk = GLA_CHUNK if seq % GLA_CHUNK == 0 else seq
    tc = _row_tile(seq, 512)
    kern = functools.partial(_gla_kernel, chunk=chunk, n_chunk=tc // chunk, dk=dk, dv=dv)
    state = pl.BlockSpec((None, heads, dv, dk), lambda b, t: (b, 0, 0, 0))
    return pl.pallas_call(
        kern,
        grid=(bsz, seq // tc),
        in_specs=[pl.BlockSpec((None, tc, d), lambda b, t: (b, t, 0)),
                  _const_spec(wz.shape), _const_spec(wa1.shape), _const_spec(wa2.shape),
                  _const_spec(ba.shape), _const_spec(gn.shape), state],
        out_specs=[pl.BlockSpec((None, tc, heads * dv), lambda b, t: (b, t, 0)), state],
        out_shape=[jax.ShapeDtypeStruct((bsz, seq, heads * dv), BF16),
                   jax.ShapeDtypeStruct(s0_t.shape, F32)],
        scratch_shapes=[pltpu.VMEM((heads, dv, dk), F32)],
        compiler_params=_params("parallel", "arbitrary"),
        name="gla",
    )(h16, wz, wa1, wa2, ba, gn, s0_t)


def _gmlp_kernel(h_ref, w_ref, g_ref, b_ref, ws_ref, bs_ref, o_ref, *v_ref, chunk, n_chunk, width):
    x = h_ref[...]
    z = _dot(x, w_ref[...])
    u = jax.nn.gelu(z[:, :width])
    v = _layer_norm(jax.nn.gelu(z[:, width:]), g_ref[...], b_ref[...])
    if v_ref:
        v_ref[0][...] = v
    gdim = width // GMLP_GROUPS
    tri = _lower_tri(chunk)
    for g in range(GMLP_GROUPS):
        w_g = jnp.where(tri, ws_ref[g], 0.0).astype(BF16)
        bias = bs_ref[:, g:g + 1]
        cols = slice(g * gdim, (g + 1) * gdim)
        for n in range(n_chunk):
            rows = slice(n * chunk, (n + 1) * chunk)
            mixed = _dot(w_g, v[rows, cols].astype(BF16)) + bias
            o_ref[rows, cols] = (u[rows, cols] * mixed).astype(o_ref.dtype)


def _gmlp(h16, w, g, b, ws, bs_t, keep_v):
    bsz, seq, d = h16.shape
    width = g.shape[-1]
    chunk = ws.shape[-1]
    tc = _row_tile(seq, 512)
    kern = functools.partial(_gmlp_kernel, chunk=chunk, n_chunk=tc // chunk, width=width)
    tile = pl.BlockSpec((None, tc, width), lambda i, t: (i, t, 0))
    out_specs = [tile]
    out_shape = [jax.ShapeDtypeStruct((bsz, seq, width), BF16)]
    if keep_v:
        out_specs.append(tile)
        out_shape.append(jax.ShapeDtypeStruct((bsz, seq, width), F32))
    return pl.pallas_call(
        kern,
        grid=(bsz, seq // tc),
        in_specs=[pl.BlockSpec((None, tc, d), lambda i, t: (i, t, 0)),
                  _const_spec(w.shape), _const_spec(g.shape), _const_spec(b.shape),
                  _const_spec(ws.shape), _const_spec(bs_t.shape)],
        out_specs=out_specs,
        out_shape=out_shape,
        compiler_params=_params("parallel", "parallel"),
        name="gmlp",
    )(h16, w, g, b, ws, bs_t)


def _transpose_bf16(x):
    r, c = x.shape
    if r % LANES == 0 and c % LANES == 0:
        return x.T
    row = lax.broadcasted_iota(jnp.int32, (r, r), 0)
    col = lax.broadcasted_iota(jnp.int32, (r, r), 1)
    return _dot_tn(x.astype(BF16), jnp.where(row == col, 1.0, 0.0).astype(BF16))


def _fox_pack(dcum, zq, zk, zv, q_ref, k_ref, v_ref, *, dh):
    tm = dcum.shape[0]
    hi = dcum.astype(BF16).astype(F32)
    r1 = dcum - hi
    mid = r1.astype(BF16).astype(F32)
    lo = (r1 - mid).astype(BF16).astype(F32)
    lane = lax.broadcasted_iota(jnp.int32, (tm, LANES), 1)
    sub = lax.broadcasted_iota(jnp.int32, (8, tm), 0)
    pad = jnp.zeros((LANES - dh - 8, tm), F32)
    ones_row = jnp.where(sub == 0, 1.0, 0.0)
    zv_t = _transpose_bf16(zv)
    if q_ref is not None:
        zq_t = _transpose_bf16(zq * (dh ** -0.5))
        hi_t, mid_t, lo_t = _transpose_bf16(hi), _transpose_bf16(mid), _transpose_bf16(lo)
    for h in range(FOX_HEADS):
        cols = slice((h // 2) * LANES, (h // 2 + 1) * LANES)
        odd = h % 2
        data = (lane >= dh) if odd else (lane < dh)
        slot = lane - (0 if odd else dh)
        d_hi = jnp.broadcast_to(hi[:, h:h + 1], (tm, LANES))
        d_mid = jnp.broadcast_to(mid[:, h:h + 1], (tm, LANES))
        d_lo = jnp.broadcast_to(lo[:, h:h + 1], (tm, LANES))
        ones_first = jnp.where(slot < 0, 0.0, jnp.where(slot < 3, 1.0, 0.0))
        k_extra = jnp.where(slot == 3, -d_hi, jnp.where(slot == 4, -d_mid, jnp.where(slot == 5, -d_lo, ones_first)))
        k_ref[h] = jnp.where(data, zk[:, cols], k_extra).astype(k_ref.dtype)
        rows = slice(h * dh, (h + 1) * dh)
        v_ref[h] = jnp.concatenate([zv_t[rows], ones_row, pad], axis=0).astype(v_ref.dtype)
        if q_ref is not None:
            bias = jnp.where(sub == 0, jnp.broadcast_to(hi_t[h:h + 1], (8, tm)),
                             jnp.where(sub == 1, jnp.broadcast_to(mid_t[h:h + 1], (8, tm)),
                                       jnp.where(sub == 2, jnp.broadcast_to(lo_t[h:h + 1], (8, tm)),
                                                 jnp.where(sub < 6, 1.0, 0.0))))
            parts = [bias, pad, zq_t[rows]] if odd else [zq_t[rows], bias, pad]
            q_ref[h] = jnp.concatenate(parts, axis=0).astype(q_ref.dtype)


def _fox_proj_kernel(h_ref, w_ref, wf_ref, bf_ref, d0_ref, *refs, width):
    kout_ref, vout_ref, lf_ref, q_ref, k_ref, v_ref, d_scr = refs[-7:]
    t = pl.program_id(1)

    @pl.when(t == 0)
    def _():
        d_scr[...] = d0_ref[...]

    tm = h_ref.shape[0]
    x = h_ref[...]
    z = _dot(x, w_ref[...])
    zq, zk, zv = z[:, :width], z[:, width:2 * width], z[:, 2 * width:]
    kout_ref[...] = zk.reshape(kout_ref.shape)
    vout_ref[...] = zv.reshape(vout_ref.shape)
    lf = _log_sigmoid(_dot(x, wf_ref[...]) + bf_ref[...])
    lane = lax.broadcasted_iota(jnp.int32, lf.shape, 1)
    lf = jnp.where(lane < FOX_HEADS, lf, 0.0)
    lf_ref[...] = lf[:, :FOX_HEADS]
    tri_bf = jnp.where(_lower_tri(tm), 1.0, 0.0).astype(BF16)
    dcum = _tri_cumsum(tri_bf, lf) + d_scr[...]
    d_scr[...] = dcum[tm - 1:tm, :]
    _fox_pack(dcum, zq, zk, zv, q_ref, k_ref, v_ref, dh=width // FOX_HEADS)


def _fox_proj(h16, w, wf, bf, d0, layer, stacked):
    bsz, seq, d = h16.shape
    width = w.shape[1] // 3
    dh = width // FOX_HEADS
    tm = _row_tile(seq, 512)
    kern = functools.partial(_fox_proj_kernel, width=width)
    heads = pl.BlockSpec((None, None, tm, FOX_HEADS, dh), lambda b, t: (layer, b, t, 0, 0))
    rows = pl.BlockSpec((None, FOX_HEADS, tm, LANES), lambda b, t: (b, 0, t, 0))
    cols = pl.BlockSpec((None, FOX_HEADS, LANES, tm), lambda b, t: (b, 0, 0, t))
    rows_shape = jax.ShapeDtypeStruct((bsz, FOX_HEADS, seq, LANES), BF16)
    cols_shape = jax.ShapeDtypeStruct((bsz, FOX_HEADS, LANES, seq), BF16)
    in_specs = [pl.BlockSpec((None, tm, d), lambda b, t: (b, t, 0)),
                _const_spec(w.shape), _const_spec(wf.shape), _const_spec(bf.shape),
                pl.BlockSpec((None, 1, LANES), lambda b, t: (b, 0, 0))]
    args = [h16, w, wf, bf, d0]
    aliases = {len(args) + i: i for i in range(len(stacked))}
    in_specs += [pl.BlockSpec(memory_space=pl.ANY)] * len(stacked)
    args += list(stacked)
    return pl.pallas_call(
        kern,
        grid=(bsz, seq // tm),
        in_specs=in_specs,
        out_specs=[heads, heads, pl.BlockSpec((None, None, tm, FOX_HEADS), lambda b, t: (layer, b, t, 0)),
                   cols, rows, cols],
        out_shape=[jax.ShapeDtypeStruct(s.shape, s.dtype) for s in stacked] + [cols_shape, rows_shape, cols_shape],
        scratch_shapes=[pltpu.VMEM((1, LANES), F32)],
        input_output_aliases=aliases,
        compiler_params=_params("parallel", "arbitrary"),
        name="fox_proj",
    )(*args)


def _fox_cache_kernel(ck_ref, cv_ref, cl_ref, k_ref, v_ref, dend_ref, d_scr, *, width):
    t = pl.program_id(1)

    @pl.when(t == 0)
    def _():
        d_scr[...] = jnp.zeros_like(d_scr)

    tm = ck_ref.shape[0]
    tri_bf = jnp.where(_lower_tri(tm), 1.0, 0.0).astype(BF16)
    dcum = _tri_cumsum(tri_bf, cl_ref[...]) + d_scr[...]
    d_scr[...] = dcum[tm - 1:tm, :]
    _fox_pack(dcum, None, ck_ref[...], cv_ref[...], None, k_ref, v_ref, dh=width // FOX_HEADS)

    @pl.when(t == pl.num_programs(1) - 1)
    def _():
        lane = lax.broadcasted_iota(jnp.int32, (1, LANES), 1)
        row = jnp.zeros((1, LANES), F32)
        for h in range(FOX_HEADS):
            row = jnp.where(lane == h, dcum[tm - 1:tm, h:h + 1], row)
        dend_ref[...] = row


def _fox_cache(ck, cv, cl):
    bsz, past, width = ck.shape
    tm = _row_tile(past, 512)
    kern = functools.partial(_fox_cache_kernel, width=width)
    tile = pl.BlockSpec((None, tm, width), lambda b, t: (b, t, 0))
    return pl.pallas_call(
        kern,
        grid=(bsz, past // tm),
        in_specs=[tile, tile, pl.BlockSpec((None, tm, FOX_HEADS), lambda b, t: (b, t, 0))],
        out_specs=[pl.BlockSpec((None, FOX_HEADS, tm, LANES), lambda b, t: (b, 0, t, 0)),
                   pl.BlockSpec((None, FOX_HEADS, LANES, tm), lambda b, t: (b, 0, 0, t)),
                   pl.BlockSpec((None, 1, LANES), lambda b, t: (b, 0, 0))],
        out_shape=[jax.ShapeDtypeStruct((bsz, FOX_HEADS, past, LANES), BF16),
                   jax.ShapeDtypeStruct((bsz, FOX_HEADS, LANES, past), BF16),
                   jax.ShapeDtypeStruct((bsz, 1, LANES), F32)],
        scratch_shapes=[pltpu.VMEM((1, FOX_HEADS), F32)],
        compiler_params=_params("parallel", "arbitrary"),
        name="fox_cache",
    )(ck, cv, cl)


def _fox_attn_kernel(q_ref, kp_ref, vp_ref, kn_ref, vn_ref, o_ref, s_scr, sd_scr, acc_scr,
                     *, tk, dh, full_base, full_per_q):
    tq = q_ref.shape[2]
    n_full = full_base + pl.program_id(2) * full_per_q
    q = [q_ref[hh] for hh in range(2)]
    causal = lax.broadcasted_iota(jnp.int32, (tq, tq), 0) <= lax.broadcasted_iota(jnp.int32, (tq, tq), 1)

    def scores(first, width):
        def step(i, ms):
            rows = pl.ds(pl.multiple_of(first + i * width, tk), width)
            out = []
            for hh in range(2):
                s = _dot(kp_ref[hh, rows, :], q[hh])
                s_scr[hh, rows, :] = s
                out.append(jnp.maximum(ms[hh], jnp.max(s, axis=0, keepdims=True)))
            return tuple(out)
        return step

    def accumulate(first, width):
        def step(i, carry):
            rows = pl.ds(pl.multiple_of(first + i * width, tk), width)
            for hh in range(2):
                p = jnp.exp(s_scr[hh, rows, :] - ms[hh]).astype(BF16)
                acc_scr[hh] += _dot(vp_ref[hh, :, rows], p)
            return carry
        return step

    runs, first, left = [], 0, n_full
    for group in ATTN_GROUPS:
        count = left // group
        runs.append((first, group * tk, count))
        first = first + count * group * tk
        left = left - count * group

    ms = tuple(jnp.full((1, tq), -jnp.inf, F32) for _ in range(2))
    for first, width, count in runs:
        ms = lax.fori_loop(0, count, scores(first, width), ms)
    ms = list(ms)
    for hh in range(2):
        s = jnp.where(causal, _dot(kn_ref[hh], q[hh]), -jnp.inf)
        sd_scr[hh] = s
        ms[hh] = jnp.maximum(ms[hh], jnp.max(s, axis=0, keepdims=True))
        acc_scr[hh] = _dot(vn_ref[hh], jnp.exp(sd_scr[hh] - ms[hh]).astype(BF16))
    for first, width, count in runs:
        lax.fori_loop(0, count, accumulate(first, width), 0)
    halves = []
    for hh in range(2):
        acc = acc_scr[hh]
        halves.append(acc[:dh] / acc[dh:dh + 1])
    o_ref[...] = _transpose_bf16(jnp.concatenate(halves, axis=0)).astype(o_ref.dtype)


def _fox_attn(q_t, kp, vp_t, kn, vn_t, *, dh):
    bsz, heads, _, seq = q_t.shape
    self_attn = kp is kn
    past = kp.shape[2]
    off = 0 if self_attn else past
    tq = _row_tile(seq, 2 * LANES)
    nq = seq // tq
    tk = tq if self_attn else _row_tile(past, 512)
    assert off % tk == 0 and (nq == 1 or tq % tk == 0), (off, tq, tk)
    full_base, full_per_q = off // tk, tq // tk
    max_full = max(full_base + (nq - 1) * full_per_q, 1)
    kern = functools.partial(_fox_attn_kernel, tk=tk, dh=dh, full_base=full_base, full_per_q=full_per_q)
    return pl.pallas_call(
        kern,
        grid=(bsz, heads // 2, nq),
        in_specs=[pl.BlockSpec((None, 2, LANES, tq), lambda b, j, i: (b, j, 0, i)),
                  pl.BlockSpec((None, 2, past, LANES), lambda b, j, i: (b, j, 0, 0)),
                  pl.BlockSpec((None, 2, LANES, past), lambda b, j, i: (b, j, 0, 0)),
                  pl.BlockSpec((None, 2, tq, LANES), lambda b, j, i: (b, j, i, 0)),
                  pl.BlockSpec((None, 2, LANES, tq), lambda b, j, i: (b, j, 0, i))],
        out_specs=pl.BlockSpec((None, tq, LANES), lambda b, j, i: (b, i, j)),
        out_shape=jax.ShapeDtypeStruct((bsz, seq, heads // 2 * LANES), BF16),
        scratch_shapes=[pltpu.VMEM((2, max_full * tk, tq), F32), pltpu.VMEM((2, tq, tq), F32),
                        pltpu.VMEM((2, LANES, tq), F32)],
        compiler_params=_params("parallel", "parallel", "arbitrary"),
        name="fox_attn",
    )(q_t, kp, vp_t, kn, vn_t)


def _merge_kernel(h32_ref, h16_ref, og_ref, om_ref, of_ref, wg_ref, wb_ref, wo_ref, lg_ref, lb_ref,
                  o32_ref, o16_ref, *, alpha):
    d = h32_ref.shape[1]
    x = h16_ref[...]
    m = None
    for r, o_ref in enumerate((og_ref, om_ref, of_ref)):
        gate = jax.nn.sigmoid(_dot(x, wg_ref[:, r * d:(r + 1) * d]))
        term = gate * _dot(o_ref[...], wb_ref[r])
        m = term if m is None else m + term
    y = alpha * h32_ref[...] + _dot(m.astype(BF16), wo_ref[...])
    y = _layer_norm(y, lg_ref[...], lb_ref[...])
    o32_ref[...] = y
    o16_ref[...] = y.astype(BF16)


def _merge(h32, h16, o_gla, o_gm, o_fox, wg, wb, wo, lg, lb, alpha):
    n, d = h32.shape
    w = o_gla.shape[1]
    tm = _row_tile(n, 256)
    row = pl.BlockSpec((tm, d), lambda i: (i, 0))
    mix = pl.BlockSpec((tm, w), lambda i: (i, 0))
    return pl.pallas_call(
        functools.partial(_merge_kernel, alpha=alpha),
        grid=(n // tm,),
        in_specs=[row, row, mix, mix, mix, _const_spec(wg.shape), _const_spec(wb.shape),
                  _const_spec(wo.shape), _const_spec(lg.shape), _const_spec(lb.shape)],
        out_specs=[row, row],
        out_shape=[jax.ShapeDtypeStruct((n, d), F32), jax.ShapeDtypeStruct((n, d), BF16)],
        compiler_params=_params("parallel"),
        name="merge",
    )(h32, h16, o_gla, o_gm, o_fox, wg, wb, wo, lg, lb)


def _ffn_kernel(h32_ref, h16_ref, wg_ref, wu_ref, wd_ref, lg_ref, lb_ref, o32_ref, o16_ref, acc_scr, *, alpha):
    j = pl.program_id(1)

    @pl.when(j == 0)
    def _():
        acc_scr[...] = jnp.zeros_like(acc_scr)

    x = h16_ref[...]
    a = _silu(_dot(x, wg_ref[...])) * _dot(x, wu_ref[...])
    acc_scr[...] += _dot(a.astype(BF16), wd_ref[...])

    @pl.when(j == pl.num_programs(1) - 1)
    def _():
        y = _layer_norm(alpha * h32_ref[...] + acc_scr[...], lg_ref[...], lb_ref[...])
        o32_ref[...] = y
        o16_ref[...] = y.astype(BF16)


def _ffn(h32, h16, wg, wu, wd, lg, lb, alpha):
    n, d = h32.shape
    f = wg.shape[1]
    tm = _row_tile(n, 1024)
    tf = _row_tile(f, 512)
    row = pl.BlockSpec((tm, d), lambda i, j: (i, 0))
    return pl.pallas_call(
        functools.partial(_ffn_kernel, alpha=alpha),
        grid=(n // tm, f // tf),
        in_specs=[row, row, pl.BlockSpec((d, tf), lambda i, j: (0, j)), pl.BlockSpec((d, tf), lambda i, j: (0, j)),
                  pl.BlockSpec((tf, d), lambda i, j: (j, 0)), _const_spec(lg.shape), _const_spec(lb.shape)],
        out_specs=[row, row],
        out_shape=[jax.ShapeDtypeStruct((n, d), F32), jax.ShapeDtypeStruct((n, d), BF16)],
        scratch_shapes=[pltpu.VMEM((tm, d), F32)],
        compiler_params=_params("parallel", "arbitrary"),
        name="ffn",
    )(h32, h16, wg, wu, wd, lg, lb)


R_W1, R_W2, R_E1, R_E2, R_RANK1, R_RANK2 = range(6)


def _route_kernel(h_ref, w_ref, r_ref, cnt_ref, cnt_scr, *, n_experts):
    @pl.when(pl.program_id(0) == 0)
    def _():
        cnt_scr[...] = jnp.zeros_like(cnt_scr)

    tm = h_ref.shape[0]
    logits = jnp.dot(h_ref[...], w_ref[...], preferred_element_type=F32, precision=lax.Precision.HIGHEST)
    lane = lax.broadcasted_iota(jnp.int32, logits.shape, 1)
    logits = jnp.where(lane < n_experts, logits, -jnp.inf)
    v1 = jnp.max(logits, axis=-1, keepdims=True)
    i1 = jnp.min(jnp.where(logits == v1, lane, LANES), axis=-1, keepdims=True)
    rest = jnp.where(lane == i1, -jnp.inf, logits)
    v2 = jnp.max(rest, axis=-1, keepdims=True)
    i2 = jnp.min(jnp.where(rest == v2, lane, LANES), axis=-1, keepdims=True)
    e2 = jnp.exp(v2 - v1)
    w1 = 1.0 / (1.0 + e2)
    w2 = e2 / (1.0 + e2)
    hot1 = lane == i1
    hot2 = lane == i2
    sel = jnp.where(hot1, 1.0, jnp.where(hot2, 1.0, 0.0))
    tri_bf = jnp.where(_lower_tri(tm), 1.0, 0.0).astype(BF16)
    incl = _dot(tri_bf, sel.astype(BF16)) + cnt_scr[...]
    excl = incl - sel
    rank1 = jnp.sum(jnp.where(hot1, excl, 0.0), axis=-1, keepdims=True)
    rank2 = jnp.sum(jnp.where(hot2, excl, 0.0), axis=-1, keepdims=True)
    cnt_scr[...] = incl[tm - 1:tm, :]
    cnt_ref[...] = incl[tm - 1:tm, :]
    rec = jnp.zeros(logits.shape, F32)
    for slot, val in ((R_W1, w1), (R_W2, w2), (R_E1, i1.astype(F32)), (R_E2, i2.astype(F32)),
                      (R_RANK1, rank1), (R_RANK2, rank2)):
        rec = jnp.where(lane == slot, val, rec)
    r_ref[...] = rec


def _route(h32, w_router):
    n, d = h32.shape
    n_experts = w_router.shape[1]
    w_pad = jnp.pad(w_router, ((0, 0), (0, LANES - n_experts)))
    tm = _row_tile(n, 512)
    return pl.pallas_call(
        functools.partial(_route_kernel, n_experts=n_experts),
        grid=(n // tm,),
        in_specs=[pl.BlockSpec((tm, d), lambda i: (i, 0)), _const_spec(w_pad.shape)],
        out_specs=[pl.BlockSpec((tm, LANES), lambda i: (i, 0)), _const_spec((1, LANES))],
        out_shape=[jax.ShapeDtypeStruct((n, LANES), F32), jax.ShapeDtypeStruct((1, LANES), F32)],
        scratch_shapes=[pltpu.VMEM((1, LANES), F32)],
        compiler_params=_params("arbitrary"),
        name="route",
    )(h32, w_pad)


def _dispatch_kernel(pos_ref, h_ref, xin_ref, xg_ref, sem):
    del xin_ref
    tm = pos_ref.shape[1]

    def row_copy(r, s):
        return pltpu.make_async_copy(h_ref.at[pl.ds(r, 1)], xg_ref.at[pl.ds(pos_ref[s, r], 1)], sem)

    def start(r, carry):
        for s in range(TOP_K):
            row_copy(r, s).start()
        return carry

    def wait(r, carry):
        for s in range(TOP_K):
            row_copy(r, s).wait()
        return carry

    lax.fori_loop(0, tm, start, 0)
    lax.fori_loop(0, tm, wait, 0)


def _dispatch(pos, h32, n_rows):
    n, d = h32.shape
    tm = _row_tile(n, 1024)
    return pl.pallas_call(
        _dispatch_kernel,
        grid=(n // tm,),
        in_specs=[pl.BlockSpec((TOP_K, tm), lambda i: (0, i), memory_space=pltpu.SMEM),
                  pl.BlockSpec((tm, d), lambda i: (i, 0)), pl.BlockSpec(memory_space=pl.ANY)],
        out_specs=pl.BlockSpec(memory_space=pl.ANY),
        out_shape=jax.ShapeDtypeStruct((n_rows, d), F32),
        scratch_shapes=[pltpu.SemaphoreType.DMA(())],
        input_output_aliases={2: 0},
        compiler_params=_params("arbitrary"),
        name="moe_dispatch",
    )(pos, h32, jnp.zeros((n_rows, d), F32))


def _group_ffn_kernel(te_ref, nt_ref, x_ref, wg_ref, wu_ref, wd_ref, y_ref, acc_scr, x16_scr):
    del te_ref
    j = pl.program_id(1)
    last = pl.num_programs(1) - 1
    live = pl.program_id(0) < nt_ref[0]

    @pl.when(live & (j == 0))
    def _():
        x16_scr[...] = x_ref[...].astype(BF16)
        acc_scr[...] = jnp.zeros_like(acc_scr)

    @pl.when(live)
    def _():
        x = x16_scr[...]
        a = _silu(_dot(x, wg_ref[...])) * _dot(x, wu_ref[...])
        acc_scr[...] += _dot(a.astype(BF16), wd_ref[...])

    @pl.when(live & (j == last))
    def _():
        y_ref[...] = acc_scr[...]

    @pl.when(jnp.logical_not(live) & (j == last))
    def _():
        y_ref[...] = jnp.zeros_like(y_ref)


def _group_ffn(tile_expert, n_tiles, xg, wg, wu, wd, tm):
    n_rows, d = xg.shape
    f = wg.shape[2]
    tf = _row_tile(f, 512)
    nf = f // tf

    def col(i, j, te, nt):
        return jnp.where(i < nt[0], j, nf - 1)

    grid_spec = pltpu.PrefetchScalarGridSpec(
        num_scalar_prefetch=2,
        grid=(n_rows // tm, nf),
        in_specs=[pl.BlockSpec((tm, d), lambda i, j, te, nt: (i, 0)),
                  pl.BlockSpec((None, d, tf), lambda i, j, te, nt: (te[i], 0, col(i, j, te, nt))),
                  pl.BlockSpec((None, d, tf), lambda i, j, te, nt: (te[i], 0, col(i, j, te, nt))),
                  pl.BlockSpec((None, tf, d), lambda i, j, te, nt: (te[i], col(i, j, te, nt), 0))],
        out_specs=pl.BlockSpec((tm, d), lambda i, j, te, nt: (i, 0)),
        scratch_shapes=[pltpu.VMEM((tm, d), F32), pltpu.VMEM((tm, d), BF16)],
    )
    return pl.pallas_call(
        _group_ffn_kernel,
        grid_spec=grid_spec,
        out_shape=jax.ShapeDtypeStruct((n_rows, d), F32),
        compiler_params=_params("arbitrary", "arbitrary"),
        name="moe_ffn",
    )(tile_expert, n_tiles, xg, wg, wu, wd)


def _combine_kernel(pos_ref, r_ref, h32_ref, y_ref, lg_ref, lb_ref, o32_ref, o16_ref, g_scr, sem, *, alpha):
    tm = h32_ref.shape[0]

    def row_copy(r, s):
        return pltpu.make_async_copy(y_ref.at[pl.ds(pos_ref[s, r], 1)], g_scr.at[s, pl.ds(r, 1)], sem)

    def start(r, carry):
        for s in range(TOP_K):
            row_copy(r, s).start()
        return carry

    def wait(r, carry):
        for s in range(TOP_K):
            row_copy(r, s).wait()
        return carry

    lax.fori_loop(0, tm, start, 0)
    lax.fori_loop(0, tm, wait, 0)
    rec = r_ref[...]
    f = rec[:, R_W1:R_W1 + 1] * g_scr[0] + rec[:, R_W2:R_W2 + 1] * g_scr[1]
    y = _layer_norm(alpha * h32_ref[...] + f, lg_ref[...], lb_ref[...])
    o32_ref[...] = y
    o16_ref[...] = y.astype(BF16)


def _combine(pos, rec, h32, yg, lg, lb, alpha):
    n, d = h32.shape
    tm = _row_tile(n, 256)
    row = pl.BlockSpec((tm, d), lambda i: (i, 0))
    return pl.pallas_call(
        functools.partial(_combine_kernel, alpha=alpha),
        grid=(n // tm,),
        in_specs=[pl.BlockSpec((TOP_K, tm), lambda i: (0, i), memory_space=pltpu.SMEM),
                  pl.BlockSpec((tm, LANES), lambda i: (i, 0)), row,
                  pl.BlockSpec(memory_space=pl.ANY), _const_spec(lg.shape), _const_spec(lb.shape)],
        out_specs=[row, row],
        out_shape=[jax.ShapeDtypeStruct((n, d), F32), jax.ShapeDtypeStruct((n, d), BF16)],
        scratch_shapes=[pltpu.VMEM((TOP_K, tm, d), F32), pltpu.SemaphoreType.DMA(())],
        compiler_params=_params("arbitrary"),
        name="moe_combine",
    )(pos, rec, h32, yg, lg, lb)


def _moe(h32, w_router, wg, wu, wd, lg, lb, alpha):
    n, d = h32.shape
    n_experts = wg.shape[0]
    tm = 1024 if n >= 8192 else 256
    max_tiles = (TOP_K * n) // tm + n_experts
    rec, counts = _route(h32, w_router)

    cnt = counts[0, :n_experts].astype(jnp.int32)
    ends = jnp.cumsum((cnt + tm - 1) // tm * tm)
    starts = ends - (cnt + tm - 1) // tm * tm
    expert = rec[:, R_E1:R_E2 + 1].astype(jnp.int32)
    rank = rec[:, R_RANK1:R_RANK2 + 1].astype(jnp.int32)
    pos = (starts[expert] + rank).T
    n_tiles = (ends[-1] // tm).astype(jnp.int32)
    tile_start = jnp.arange(max_tiles, dtype=jnp.int32) * tm
    tile_expert = jnp.minimum(jnp.searchsorted(ends, tile_start, side='right'), n_experts - 1).astype(jnp.int32)
    tile_expert = jnp.where(jnp.arange(max_tiles) < n_tiles, tile_expert, tile_expert[jnp.maximum(n_tiles - 1, 0)])

    xg = _dispatch(pos, h32, max_tiles * tm)
    yg = _group_ffn(tile_expert, n_tiles.reshape(1), xg, wg, wu, wd, tm)
    return _combine(pos, rec, h32, yg, lg, lb, alpha)


def _layer_weights(w, l, d, mix_w):
    rank = w['w_a2'].shape[1]
    qk_w = w['w_a2'].shape[2]
    sizes = (qk_w, qk_w, mix_w, mix_w, rank, mix_w, mix_w, mix_w, FOX_HEADS, mix_w, mix_w, 3 * d)
    offs = [0]
    for s in sizes:
        offs.append(offs[-1] + s)
    w_in = w['w_in'][l]
    col = lambda a, b: w_in[:, offs[a]:offs[b]]
    pad_cols = lambda x: jnp.pad(x, ((0, 0), (0, LANES - x.shape[1])))
    return dict(
        gla_wz=col(0, 4).astype(BF16),
        gla_wa1=pad_cols(col(4, 5)).astype(BF16),
        gla_wa2=jnp.pad(w['w_a2'][l], ((0, LANES - rank), (0, 0))).astype(BF16),
        gla_ba=w['b_a'][l].reshape(1, -1),
        gla_gn=w['gla_norm_g'][l].reshape(1, -1),
        fox_w=col(5, 8).astype(BF16),
        fox_wf=pad_cols(col(8, 9)).astype(BF16),
        fox_bf=pad_cols(w['b_f'][l].reshape(1, -1)),
        gm_w=col(9, 11).astype(BF16),
        gm_g=w['gmlp_norm_g'][l].reshape(1, -1),
        gm_b=w['gmlp_norm_b'][l].reshape(1, -1),
        gate_w=col(11, 12).astype(BF16),
        branch_w=w['w_branch'][l].astype(BF16),
        out_w=w['w_out'][l].astype(BF16),
    )


def _trunk(x, w, gla_s, fox_k, fox_v, fox_lf, keep_gmlp_rows):
    bsz, seq, d = x.shape
    depth = w['w_in'].shape[0]
    mix_w = w['gla_norm_g'].shape[1]
    alpha = (2 * depth) ** 0.25
    n = bsz * seq
    dk = w['w_a2'].shape[2] // GLA_HEADS
    dv = mix_w // GLA_HEADS
    dh = mix_w // FOX_HEADS
    gm_chunk = GMLP_CHUNK if seq % GMLP_CHUNK == 0 else seq

    h32, h16 = _ln_in(x.reshape(n, d), w['ln_in_g'], w['ln_in_b'])
    s_out, gm_out = [], []
    fox_out = [jnp.zeros((depth, bsz, seq, FOX_HEADS, dh), F32), jnp.zeros((depth, bsz, seq, FOX_HEADS, dh), F32),
               jnp.zeros((depth, bsz, seq, FOX_HEADS), F32)]
    for l in range(depth):
        lw = _layer_weights(w, l, d, mix_w)
        h16_seq = h16.reshape(bsz, seq, d)

        if gla_s is None:
            s0_t = jnp.zeros((bsz, GLA_HEADS, dv, dk), F32)
        else:
            s0_t = jnp.swapaxes(gla_s[l], -1, -2)
        o_gla, s_t = _gla(h16_seq, lw['gla_wz'], lw['gla_wa1'], lw['gla_wa2'], lw['gla_ba'], lw['gla_gn'], s0_t)
        s_out.append(jnp.swapaxes(s_t, -1, -2))

        gm = _gmlp(h16_seq, lw['gm_w'], lw['gm_g'], lw['gm_b'], w['gmlp_ws'][l][:, :gm_chunk, :gm_chunk],
                   w['gmlp_bs'][l][:, :gm_chunk].T, keep_gmlp_rows)
        o_gm = gm[0]
        if keep_gmlp_rows:
            gm_out.append(gm[1])

        if fox_k is None:
            d0 = jnp.zeros((bsz, 1, LANES), F32)
        else:
            past = fox_k.shape[2]
            kp, vp_t, d0 = _fox_cache(fox_k[l].reshape(bsz, past, mix_w), fox_v[l].reshape(bsz, past, mix_w),
                                      fox_lf[l])
        *fox_out, q_t, kn, vn_t = _fox_proj(h16_seq, lw['fox_w'], lw['fox_wf'], lw['fox_bf'], d0, l, fox_out)
        if fox_k is None:
            kp, vp_t = kn, vn_t
        o_fox = _fox_attn(q_t, kp, vp_t, kn, vn_t, dh=dh)

        h32, h16 = _merge(h32, h16, o_gla.reshape(n, mix_w), o_gm.reshape(n, mix_w), o_fox.reshape(n, mix_w),
                          lw['gate_w'], lw['branch_w'], lw['out_w'],
                          w['ln_g'][l, 0].reshape(1, d), w['ln_b'][l, 0].reshape(1, d), alpha)
        lg, lb = w['ln_g'][l, 1].reshape(1, d), w['ln_b'][l, 1].reshape(1, d)
        j = l // 2
        if l % 2 == 0:
            h32, h16 = _ffn(h32, h16, w['ffn_w_gate'][j].astype(BF16), w['ffn_w_up'][j].astype(BF16),
                            w['ffn_w_down'][j].astype(BF16), lg, lb, alpha)
        else:
            h32, h16 = _moe(h32, w['moe_router'][j], w['moe_w_gate'][j].astype(BF16),
                            w['moe_w_up'][j].astype(BF16), w['moe_w_down'][j].astype(BF16), lg, lb, alpha)
    gm_stack = jnp.stack(gm_out) if keep_gmlp_rows else None
    return (h32.reshape(bsz, seq, d), jnp.stack(s_out), *fox_out, gm_stack)


def kernel(x_prompt, x_sample, state_gla, cache_fox_k, cache_fox_v, cache_fox_logf, ln_in_g, ln_in_b, w_in, w_a2, b_a, gla_norm_g, b_f, gmlp_norm_g, gmlp_norm_b, gmlp_ws, gmlp_bs, w_branch, w_out, ln_g, ln_b, ffn_w_gate, ffn_w_up, ffn_w_down, moe_router, moe_w_gate, moe_w_up, moe_w_down):
    w = dict(ln_in_g=ln_in_g, ln_in_b=ln_in_b, w_in=w_in, w_a2=w_a2, b_a=b_a, gla_norm_g=gla_norm_g,
             b_f=b_f, gmlp_norm_g=gmlp_norm_g, gmlp_norm_b=gmlp_norm_b, gmlp_ws=gmlp_ws, gmlp_bs=gmlp_bs,
             w_branch=w_branch, w_out=w_out, ln_g=ln_g, ln_b=ln_b, ffn_w_gate=ffn_w_gate,
             ffn_w_up=ffn_w_up, ffn_w_down=ffn_w_down, moe_router=moe_router, moe_w_gate=moe_w_gate,
             moe_w_up=moe_w_up, moe_w_down=moe_w_down)
    y_p, gla_p, fk_p, fv_p, flf_p, _ = _trunk(x_prompt, w, None, None, None, None, False)
    y_s, gla_s, fk_s, fv_s, flf_s, gmv_s = _trunk(x_sample, w, state_gla, cache_fox_k, cache_fox_v,
                                                  cache_fox_logf, True)
    return (y_p, y_s, gla_p, fk_p, fv_p, flf_p, gla_s, fk_s, fv_s, flf_s, gmv_s)
```

```python
import functools

import jax
import jax.numpy as jnp
from jax import lax
from jax.experimental import pallas as pl
from jax.experimental.pallas import tpu as pltpu

F32 = jnp.float32
BF16 = jnp.bfloat16

GLA_HEADS = 4
GLA_CHUNK = 64
GLA_TAU = 16.0
FOX_HEADS = 8
GMLP_GROUPS = 4
GMLP_CHUNK = 128
TOP_K = 2
ATTN_GROUPS = (8, 4, 2, 1)
DMA_UNROLL = 8
LN_EPS = 1e-5
RMS_EPS = 1e-6

V7X_VMEM_BYTES = 64 * 1024 * 1024
VMEM_LIMIT_BYTES = V7X_VMEM_BYTES * 3 // 4
LANES = 128


def _params(*semantics, flags=None):
    return pltpu.CompilerParams(dimension_semantics=semantics, vmem_limit_bytes=VMEM_LIMIT_BYTES, flags=flags)


def _const_spec(shape):
    zeros = (0,) * len(shape)
    return pl.BlockSpec(shape, lambda *_: zeros)


def _dot(a, b):
    return jnp.dot(a, b, preferred_element_type=F32)


def _dot_nt(a, b):
    return lax.dot_general(a, b, (((1,), (1,)), ((), ())), preferred_element_type=F32)


def _dot_tn(a, b):
    return lax.dot_general(a, b, (((0,), (0,)), ((), ())), preferred_element_type=F32)


def _layer_norm(x, g, b):
    mu = jnp.mean(x, axis=-1, keepdims=True)
    xc = x - mu
    var = jnp.mean(xc * xc, axis=-1, keepdims=True)
    return xc * lax.rsqrt(var + LN_EPS) * g + b


def _log_sigmoid(x):
    return -(jnp.maximum(-x, 0.0) + jnp.log1p(jnp.exp(-jnp.abs(x))))


def _silu(x):
    return x * jax.nn.sigmoid(x)


def _split3(x):
    hi = x.astype(BF16)
    r1 = x - hi.astype(F32)
    mid = r1.astype(BF16)
    lo = (r1 - mid.astype(F32)).astype(BF16)
    return hi, mid, lo


def _tri_cumsum(tri_bf, x):
    hi, mid, lo = _split3(x)
    return _dot(tri_bf, hi) + _dot(tri_bf, mid) + _dot(tri_bf, lo)


def _lower_tri(n):
    row = lax.broadcasted_iota(jnp.int32, (n, n), 0)
    col = lax.broadcasted_iota(jnp.int32, (n, n), 1)
    return row >= col


def _row_tile(n, cap):
    t = min(n, cap)
    assert n % t == 0, (n, cap)
    return t


def _ln_in_kernel(x_ref, g_ref, b_ref, o32_ref, o16_ref):
    y = _layer_norm(x_ref[...], g_ref[...], b_ref[...])
    o32_ref[...] = y
    o16_ref[...] = y.astype(BF16)


def _ln_in(x2, g, b):
    n, d = x2.shape
    tm = _row_tile(n, 512)
    row = pl.BlockSpec((tm, d), lambda i: (i, 0))
    return pl.pallas_call(
        _ln_in_kernel,
        grid=(n // tm,),
        in_specs=[row, _const_spec((1, d)), _const_spec((1, d))],
        out_specs=[row, row],
        out_shape=[jax.ShapeDtypeStruct((n, d), F32), jax.ShapeDtypeStruct((n, d), BF16)],
        compiler_params=_params("parallel"),
        name="ln_in",
    )(x2, g.reshape(1, d), b.reshape(1, d))


def _gla_kernel(h_ref, wz_ref, wa1_ref, wa2_ref, ba_ref, gn_ref, s0_ref, o_ref, sfin_ref, s_scr,
                *, chunk, n_chunk, dk, dv):
    t = pl.program_id(1)

    @pl.when(t == 0)
    def _():
        s_scr[...] = s0_ref[...]

    qk_w = GLA_HEADS * dk
    v_w = GLA_HEADS * dv
    x = h_ref[...]
    z = _dot(x, wz_ref[...])
    a1 = _dot(x, wa1_ref[...])
    log_a = _log_sigmoid(_dot(a1.astype(BF16), wa2_ref[...]) + ba_ref[...]) * (1.0 / GLA_TAU)

    tri = _lower_tri(chunk)
    tri_bf = jnp.where(tri, 1.0, 0.0).astype(BF16)
    for c in range(n_chunk):
        rows = slice(c * chunk, (c + 1) * chunk)
        b = _tri_cumsum(tri_bf, log_a[rows])
        b_last = b[chunk - 1:chunk, :]
        q = z[rows, 0:qk_w]
        k = z[rows, qk_w:2 * qk_w]
        v = z[rows, 2 * qk_w:2 * qk_w + v_w]
        r = z[rows, 2 * qk_w + v_w:2 * qk_w + 2 * v_w]
        qe = (q * (dk ** -0.5)) * jnp.exp(b)
        ke = k * jnp.exp(-b)
        kd = k * jnp.exp(b_last - b)
        decay = jnp.exp(b_last)
        for hd in range(GLA_HEADS):
            kl = slice(hd * dk, (hd + 1) * dk)
            vl = slice(hd * dv, (hd + 1) * dv)
            qe_h = qe[:, kl].astype(BF16)
            v_h = v[:, vl].astype(BF16)
            a = jnp.where(tri, _dot_nt(qe_h, ke[:, kl].astype(BF16)), 0.0)
            s_t = s_scr[hd]
            o = _dot(a.astype(BF16), v_h) + _dot_nt(qe_h, s_t.astype(BF16))
            s_scr[hd] = s_t * decay[:, kl] + _dot_tn(v_h, kd[:, kl].astype(BF16))
            o = o * lax.rsqrt(jnp.mean(o * o, axis=-1, keepdims=True) + RMS_EPS) * gn_ref[:, vl]
            o_ref[rows, vl] = (o * _silu(r[:, vl])).astype(o_ref.dtype)

    @pl.when(t == pl.num_programs(1) - 1)
    def _():
        sfin_ref[...] = s_scr[...]


def _gla(h16, wz, wa1, wa2, ba, gn, s0_t):
    bsz, seq, d = h16.shape
    _, heads, dv, dk = s0_t.shape
    chunk = GLA_CHUNK if seq % GLA_CHUNK == 0 else seq
    tc = _row_tile(seq, 512)
    kern = functools.partial(_gla_kernel, chunk=chunk, n_chunk=tc // chunk, dk=dk, dv=dv)
    state = pl.BlockSpec((None, heads, dv, dk), lambda b, t: (b, 0, 0, 0))
    return pl.pallas_call(
        kern,
        grid=(bsz, seq // tc),
        in_specs=[pl.BlockSpec((None, tc, d), lambda b, t: (b, t, 0)),
                  _const_spec(wz.shape), _const_spec(wa1.shape), _const_spec(wa2.shape),
                  _const_spec(ba.shape), _const_spec(gn.shape), state],
        out_specs=[pl.BlockSpec((None, tc, heads * dv), lambda b, t: (b, t, 0)), state],
        out_shape=[jax.ShapeDtypeStruct((bsz, seq, heads * dv), BF16),
                   jax.ShapeDtypeStruct(s0_t.shape, F32)],
        scratch_shapes=[pltpu.VMEM((heads, dv, dk), F32)],
        compiler_params=_params("parallel", "arbitrary"),
        name="gla",
    )(h16, wz, wa1, wa2, ba, gn, s0_t)


def _gmlp_kernel(h_ref, w_ref, g_ref, b_ref, ws_ref, bs_ref, o_ref, *v_ref, chunk, n_chunk, width):
    x = h_ref[...]
    z = _dot(x, w_ref[...])
    u = jax.nn.gelu(z[:, :width])
    v = _layer_norm(jax.nn.gelu(z[:, width:]), g_ref[...], b_ref[...])
    if v_ref:
        v_ref[0][...] = v
    gdim = width // GMLP_GROUPS
    tri = _lower_tri(chunk)
    for g in range(GMLP_GROUPS):
        w_g = jnp.where(tri, ws_ref[g], 0.0).astype(BF16)
        bias = bs_ref[:, g:g + 1]
        cols = slice(g * gdim, (g + 1) * gdim)
        for n in range(n_chunk):
            rows = slice(n * chunk, (n + 1) * chunk)
            mixed = _dot(w_g, v[rows, cols].astype(BF16)) + bias
            o_ref[rows, cols] = (u[rows, cols] * mixed).astype(o_ref.dtype)


def _gmlp(h16, w, g, b, ws, bs_t, keep_v):
    bsz, seq, d = h16.shape
    width = g.shape[-1]
    chunk = ws.shape[-1]
    tc = _row_tile(seq, 512)
    kern = functools.partial(_gmlp_kernel, chunk=chunk, n_chunk=tc // chunk, width=width)
    tile = pl.BlockSpec((None, tc, width), lambda i, t: (i, t, 0))
    out_specs = [tile]
    out_shape = [jax.ShapeDtypeStruct((bsz, seq, width), BF16)]
    if keep_v:
        out_specs.append(tile)
        out_shape.append(jax.ShapeDtypeStruct((bsz, seq, width), F32))
    return pl.pallas_call(
        kern,
        grid=(bsz, seq // tc),
        in_specs=[pl.BlockSpec((None, tc, d), lambda i, t: (i, t, 0)),
                  _const_spec(w.shape), _const_spec(g.shape), _const_spec(b.shape),
                  _const_spec(ws.shape), _const_spec(bs_t.shape)],
        out_specs=out_specs,
        out_shape=out_shape,
        compiler_params=_params("parallel", "parallel"),
        name="gmlp",
    )(h16, w, g, b, ws, bs_t)


def _transpose_bf16(x):
    r, c = x.shape
    if r % LANES == 0 and c % LANES == 0:
        return x.T
    row = lax.broadcasted_iota(jnp.int32, (r, r), 0)
    col = lax.broadcasted_iota(jnp.int32, (r, r), 1)
    return _dot_tn(x.astype(BF16), jnp.where(row == col, 1.0, 0.0).astype(BF16))


def _fox_pack(dcum, zq, zk, zv, q_ref, k_ref, v_ref, *, dh):
    tm = dcum.shape[0]
    hi = dcum.astype(BF16).astype(F32)
    r1 = dcum - hi
    mid = r1.astype(BF16).astype(F32)
    lo = (r1 - mid).astype(BF16).astype(F32)
    lane = lax.broadcasted_iota(jnp.int32, (tm, LANES), 1)
    sub = lax.broadcasted_iota(jnp.int32, (8, tm), 0)
    pad = jnp.zeros((LANES - dh - 8, tm), F32)
    ones_row = jnp.where(sub == 0, 1.0, 0.0)
    zv_t = _transpose_bf16(zv)
    if q_ref is not None:
        zq_t = _transpose_bf16(zq * (dh ** -0.5))
        hi_t, mid_t, lo_t = _transpose_bf16(hi), _transpose_bf16(mid), _transpose_bf16(lo)
    for h in range(FOX_HEADS):
        cols = slice((h // 2) * LANES, (h // 2 + 1) * LANES)
        odd = h % 2
        data = (lane >= dh) if odd else (lane < dh)
        slot = lane - (0 if odd else dh)
        d_hi = jnp.broadcast_to(hi[:, h:h + 1], (tm, LANES))
        d_mid = jnp.broadcast_to(mid[:, h:h + 1], (tm, LANES))
        d_lo = jnp.broadcast_to(lo[:, h:h + 1], (tm, LANES))
        ones_first = jnp.where(slot < 0, 0.0, jnp.where(slot < 3, 1.0, 0.0))
        k_extra = jnp.where(slot == 3, -d_hi, jnp.where(slot == 4, -d_mid, jnp.where(slot == 5, -d_lo, ones_first)))
        k_ref[h] = jnp.where(data, zk[:, cols], k_extra).astype(k_ref.dtype)
        rows = slice(h * dh, (h + 1) * dh)
        v_ref[h] = jnp.concatenate([zv_t[rows], ones_row, pad], axis=0).astype(v_ref.dtype)
        if q_ref is not None:
            bias = jnp.where(sub == 0, jnp.broadcast_to(hi_t[h:h + 1], (8, tm)),
                             jnp.where(sub == 1, jnp.broadcast_to(mid_t[h:h + 1], (8, tm)),
                                       jnp.where(sub == 2, jnp.broadcast_to(lo_t[h:h + 1], (8, tm)),
                                                 jnp.where(sub < 6, 1.0, 0.0))))
            parts = [bias, pad, zq_t[rows]] if odd else [zq_t[rows], bias, pad]
            q_ref[h] = jnp.concatenate(parts, axis=0).astype(q_ref.dtype)


def _fox_proj_kernel(h_ref, w_ref, wf_ref, bf_ref, d0_ref, *refs, width):
    kout_ref, vout_ref, lf_ref, q_ref, k_ref, v_ref, d_scr = refs[-7:]
    t = pl.program_id(1)

    @pl.when(t == 0)
    def _():
        d_scr[...] = d0_ref[...]

    tm = h_ref.shape[0]
    x = h_ref[...]
    z = _dot(x, w_ref[...])
    zq, zk, zv = z[:, :width], z[:, width:2 * width], z[:, 2 * width:]
    kout_ref[...] = zk.reshape(kout_ref.shape)
    vout_ref[...] = zv.reshape(vout_ref.shape)
    lf = _log_sigmoid(_dot(x, wf_ref[...]) + bf_ref[...])
    lane = lax.broadcasted_iota(jnp.int32, lf.shape, 1)
    lf = jnp.where(lane < FOX_HEADS, lf, 0.0)
    lf_ref[...] = lf[:, :FOX_HEADS]
    tri_bf = jnp.where(_lower_tri(tm), 1.0, 0.0).astype(BF16)
    dcum = _tri_cumsum(tri_bf, lf) + d_scr[...]
    d_scr[...] = dcum[tm - 1:tm, :]
    _fox_pack(dcum, zq, zk, zv, q_ref, k_ref, v_ref, dh=width // FOX_HEADS)


def _fox_proj(h16, w, wf, bf, d0, layer, stacked):
    bsz, seq, d = h16.shape
    width = w.shape[1] // 3
    dh = width // FOX_HEADS
    tm = _row_tile(seq, 512)
    kern = functools.partial(_fox_proj_kernel, width=width)
    heads = pl.BlockSpec((None, None, tm, FOX_HEADS, dh), lambda b, t: (layer, b, t, 0, 0))
    rows = pl.BlockSpec((None, FOX_HEADS, tm, LANES), lambda b, t: (b, 0, t, 0))
    cols = pl.BlockSpec((None, FOX_HEADS, LANES, tm), lambda b, t: (b, 0, 0, t))
    rows_shape = jax.ShapeDtypeStruct((bsz, FOX_HEADS, seq, LANES), BF16)
    cols_shape = jax.ShapeDtypeStruct((bsz, FOX_HEADS, LANES, seq), BF16)
    in_specs = [pl.BlockSpec((None, tm, d), lambda b, t: (b, t, 0)),
                _const_spec(w.shape), _const_spec(wf.shape), _const_spec(bf.shape),
                pl.BlockSpec((None, 1, LANES), lambda b, t: (b, 0, 0))]
    args = [h16, w, wf, bf, d0]
    aliases = {len(args) + i: i for i in range(len(stacked))}
    in_specs += [pl.BlockSpec(memory_space=pl.ANY)] * len(stacked)
    args += list(stacked)
    return pl.pallas_call(
        kern,
        grid=(bsz, seq // tm),
        in_specs=in_specs,
        out_specs=[heads, heads, pl.BlockSpec((None, None, tm, FOX_HEADS), lambda b, t: (layer, b, t, 0)),
                   cols, rows, cols],
        out_shape=[jax.ShapeDtypeStruct(s.shape, s.dtype) for s in stacked] + [cols_shape, rows_shape, cols_shape],
        scratch_shapes=[pltpu.VMEM((1, LANES), F32)],
        input_output_aliases=aliases,
        compiler_params=_params("parallel", "arbitrary"),
        name="fox_proj",
    )(*args)


def _fox_cache_kernel(ck_ref, cv_ref, cl_ref, k_ref, v_ref, dend_ref, d_scr, *, width):
    t = pl.program_id(1)

    @pl.when(t == 0)
    def _():
        d_scr[...] = jnp.zeros_like(d_scr)

    tm = ck_ref.shape[0]
    tri_bf = jnp.where(_lower_tri(tm), 1.0, 0.0).astype(BF16)
    dcum = _tri_cumsum(tri_bf, cl_ref[...]) + d_scr[...]
    d_scr[...] = dcum[tm - 1:tm, :]
    _fox_pack(dcum, None, ck_ref[...].reshape(tm, width), cv_ref[...].reshape(tm, width), None, k_ref, v_ref,
              dh=width // FOX_HEADS)

    @pl.when(t == pl.num_programs(1) - 1)
    def _():
        lane = lax.broadcasted_iota(jnp.int32, (1, LANES), 1)
        row = jnp.zeros((1, LANES), F32)
        for h in range(FOX_HEADS):
            row = jnp.where(lane == h, dcum[tm - 1:tm, h:h + 1], row)
        dend_ref[...] = row


def _fox_cache(ck, cv, cl, layer):
    _, bsz, past, heads, dh = ck.shape
    width = heads * dh
    tm = _row_tile(past, 512)
    kern = functools.partial(_fox_cache_kernel, width=width)
    tile = pl.BlockSpec((None, None, tm, heads, dh), lambda b, t: (layer, b, t, 0, 0))
    return pl.pallas_call(
        kern,
        grid=(bsz, past // tm),
        in_specs=[tile, tile, pl.BlockSpec((None, None, tm, heads), lambda b, t: (layer, b, t, 0))],
        out_specs=[pl.BlockSpec((None, FOX_HEADS, tm, LANES), lambda b, t: (b, 0, t, 0)),
                   pl.BlockSpec((None, FOX_HEADS, LANES, tm), lambda b, t: (b, 0, 0, t)),
                   pl.BlockSpec((None, 1, LANES), lambda b, t: (b, 0, 0))],
        out_shape=[jax.ShapeDtypeStruct((bsz, FOX_HEADS, past, LANES), BF16),
                   jax.ShapeDtypeStruct((bsz, FOX_HEADS, LANES, past), BF16),
                   jax.ShapeDtypeStruct((bsz, 1, LANES), F32)],
        scratch_shapes=[pltpu.VMEM((1, FOX_HEADS), F32)],
        compiler_params=_params("parallel", "arbitrary"),
        name="fox_cache",
    )(ck, cv, cl)


def _fox_attn_kernel(q_ref, kp_ref, vp_ref, kn_ref, vn_ref, o_ref, s_scr, sd_scr, acc_scr,
                     *, tk, dh, full_base, full_per_q):
    tq = q_ref.shape[2]
    n_full = full_base + pl.program_id(2) * full_per_q
    q = [q_ref[hh] for hh in range(2)]
    causal = lax.broadcasted_iota(jnp.int32, (tq, tq), 0) <= lax.broadcasted_iota(jnp.int32, (tq, tq), 1)

    def scores(first, width):
        def step(i, ms):
            rows = pl.ds(pl.multiple_of(first + i * width, tk), width)
            out = []
            for hh in range(2):
                s = _dot(kp_ref[hh, rows, :], q[hh])
                s_scr[hh, rows, :] = s
                out.append(jnp.maximum(ms[hh], jnp.max(s, axis=0, keepdims=True)))
            return tuple(out)
        return step

    def accumulate(first, width):
        def step(i, carry):
            rows = pl.ds(pl.multiple_of(first + i * width, tk), width)
            for hh in range(2):
                p = jnp.exp(s_scr[hh, rows, :] - ms[hh]).astype(BF16)
                acc_scr[hh] += _dot(vp_ref[hh, :, rows], p)
            return carry
        return step

    runs, first, left = [], 0, n_full
    for group in ATTN_GROUPS:
        count = left // group
        runs.append((first, group * tk, count))
        first = first + count * group * tk
        left = left - count * group

    ms = tuple(jnp.full((1, tq), -jnp.inf, F32) for _ in range(2))
    for first, width, count in runs:
        ms = lax.fori_loop(0, count, scores(first, width), ms)
    ms = list(ms)
    for hh in range(2):
        s = jnp.where(causal, _dot(kn_ref[hh], q[hh]), -jnp.inf)
        sd_scr[hh] = s
        ms[hh] = jnp.maximum(ms[hh], jnp.max(s, axis=0, keepdims=True))
        acc_scr[hh] = _dot(vn_ref[hh], jnp.exp(sd_scr[hh] - ms[hh]).astype(BF16))
    for first, width, count in runs:
        lax.fori_loop(0, count, accumulate(first, width), 0)
    halves = []
    for hh in range(2):
        acc = acc_scr[hh]
        halves.append(acc[:dh] / acc[dh:dh + 1])
    o_ref[...] = _transpose_bf16(jnp.concatenate(halves, axis=0)).astype(o_ref.dtype)


def _fox_attn(q_t, kp, vp_t, kn, vn_t, *, dh):
    bsz, heads, _, seq = q_t.shape
    self_attn = kp is kn
    past = kp.shape[2]
    off = 0 if self_attn else past
    tq = _row_tile(seq, 2 * LANES)
    nq = seq // tq
    tk = tq if self_attn else _row_tile(past, 512)
    assert off % tk == 0 and (nq == 1 or tq % tk == 0), (off, tq, tk)
    full_base, full_per_q = off // tk, tq // tk
    max_full = max(full_base + (nq - 1) * full_per_q, 1)
    kern = functools.partial(_fox_attn_kernel, tk=tk, dh=dh, full_base=full_base, full_per_q=full_per_q)
    return pl.pallas_call(
        kern,
        grid=(bsz, heads // 2, nq),
        in_specs=[pl.BlockSpec((None, 2, LANES, tq), lambda b, j, i: (b, j, 0, i)),
                  pl.BlockSpec((None, 2, past, LANES), lambda b, j, i: (b, j, 0, 0)),
                  pl.BlockSpec((None, 2, LANES, past), lambda b, j, i: (b, j, 0, 0)),
                  pl.BlockSpec((None, 2, tq, LANES), lambda b, j, i: (b, j, i, 0)),
                  pl.BlockSpec((None, 2, LANES, tq), lambda b, j, i: (b, j, 0, i))],
        out_specs=pl.BlockSpec((None, tq, LANES), lambda b, j, i: (b, i, j)),
        out_shape=jax.ShapeDtypeStruct((bsz, seq, heads // 2 * LANES), BF16),
        scratch_shapes=[pltpu.VMEM((2, max_full * tk, tq), F32), pltpu.VMEM((2, tq, tq), F32),
                        pltpu.VMEM((2, LANES, tq), F32)],
        compiler_params=_params("parallel", "parallel", "arbitrary"),
        name="fox_attn",
    )(q_t, kp, vp_t, kn, vn_t)


def _merge_kernel(h32_ref, h16_ref, og_ref, om_ref, of_ref, wg_ref, wb_ref, wo_ref, lg_ref, lb_ref,
                  o32_ref, o16_ref, *, alpha):
    d = h32_ref.shape[1]
    x = h16_ref[...]
    m = None
    for r, o_ref in enumerate((og_ref, om_ref, of_ref)):
        gate = jax.nn.sigmoid(_dot(x, wg_ref[:, r * d:(r + 1) * d]))
        term = gate * _dot(o_ref[...], wb_ref[r])
        m = term if m is None else m + term
    y = alpha * h32_ref[...] + _dot(m.astype(BF16), wo_ref[...])
    y = _layer_norm(y, lg_ref[...], lb_ref[...])
    o32_ref[...] = y
    o16_ref[...] = y.astype(BF16)


def _merge(h32, h16, o_gla, o_gm, o_fox, wg, wb, wo, lg, lb, alpha):
    n, d = h32.shape
    w = o_gla.shape[1]
    tm = _row_tile(n, 256)
    row = pl.BlockSpec((tm, d), lambda i: (i, 0))
    mix = pl.BlockSpec((tm, w), lambda i: (i, 0))
    return pl.pallas_call(
        functools.partial(_merge_kernel, alpha=alpha),
        grid=(n // tm,),
        in_specs=[row, row, mix, mix, mix, _const_spec(wg.shape), _const_spec(wb.shape),
                  _const_spec(wo.shape), _const_spec(lg.shape), _const_spec(lb.shape)],
        out_specs=[row, row],
        out_shape=[jax.ShapeDtypeStruct((n, d), F32), jax.ShapeDtypeStruct((n, d), BF16)],
        compiler_params=_params("parallel"),
        name="merge",
    )(h32, h16, o_gla, o_gm, o_fox, wg, wb, wo, lg, lb)


def _ffn_kernel(h32_ref, h16_ref, wg_ref, wu_ref, wd_ref, lg_ref, lb_ref, o32_ref, o16_ref, acc_scr, *, alpha):
    j = pl.program_id(1)

    @pl.when(j == 0)
    def _():
        acc_scr[...] = jnp.zeros_like(acc_scr)

    x = h16_ref[...]
    a = _silu(_dot(x, wg_ref[...])) * _dot(x, wu_ref[...])
    acc_scr[...] += _dot(a.astype(BF16), wd_ref[...])

    @pl.when(j == pl.num_programs(1) - 1)
    def _():
        y = _layer_norm(alpha * h32_ref[...] + acc_scr[...], lg_ref[...], lb_ref[...])
        o32_ref[...] = y
        o16_ref[...] = y.astype(BF16)


def _ffn(h32, h16, wg, wu, wd, lg, lb, alpha):
    n, d = h32.shape
    f = wg.shape[1]
    tm = _row_tile(n, 1024)
    tf = _row_tile(f, 512)
    row = pl.BlockSpec((tm, d), lambda i, j: (i, 0))
    return pl.pallas_call(
        functools.partial(_ffn_kernel, alpha=alpha),
        grid=(n // tm, f // tf),
        in_specs=[row, row, pl.BlockSpec((d, tf), lambda i, j: (0, j)), pl.BlockSpec((d, tf), lambda i, j: (0, j)),
                  pl.BlockSpec((tf, d), lambda i, j: (j, 0)), _const_spec(lg.shape), _const_spec(lb.shape)],
        out_specs=[row, row],
        out_shape=[jax.ShapeDtypeStruct((n, d), F32), jax.ShapeDtypeStruct((n, d), BF16)],
        scratch_shapes=[pltpu.VMEM((tm, d), F32)],
        compiler_params=_params("parallel", "arbitrary"),
        name="ffn",
    )(h32, h16, wg, wu, wd, lg, lb)


R_W1, R_W2, R_E1, R_E2, R_RANK1, R_RANK2 = range(6)


def _route_kernel(h_ref, w_ref, r_ref, cnt_ref, cnt_scr, *, n_experts):
    @pl.when(pl.program_id(0) == 0)
    def _():
        cnt_scr[...] = jnp.zeros_like(cnt_scr)

    tm = h_ref.shape[0]
    logits = jnp.dot(h_ref[...], w_ref[...], preferred_element_type=F32, precision=lax.Precision.HIGHEST)
    lane = lax.broadcasted_iota(jnp.int32, logits.shape, 1)
    logits = jnp.where(lane < n_experts, logits, -jnp.inf)
    v1 = jnp.max(logits, axis=-1, keepdims=True)
    i1 = jnp.min(jnp.where(logits == v1, lane, LANES), axis=-1, keepdims=True)
    rest = jnp.where(lane == i1, -jnp.inf, logits)
    v2 = jnp.max(rest, axis=-1, keepdims=True)
    i2 = jnp.min(jnp.where(rest == v2, lane, LANES), axis=-1, keepdims=True)
    e2 = jnp.exp(v2 - v1)
    w1 = 1.0 / (1.0 + e2)
    w2 = e2 / (1.0 + e2)
    hot1 = lane == i1
    hot2 = lane == i2
    sel = jnp.where(hot1, 1.0, jnp.where(hot2, 1.0, 0.0))
    tri_bf = jnp.where(_lower_tri(tm), 1.0, 0.0).astype(BF16)
    incl = _dot(tri_bf, sel.astype(BF16)) + cnt_scr[...]
    excl = incl - sel
    rank1 = jnp.sum(jnp.where(hot1, excl, 0.0), axis=-1, keepdims=True)
    rank2 = jnp.sum(jnp.where(hot2, excl, 0.0), axis=-1, keepdims=True)
    cnt_scr[...] = incl[tm - 1:tm, :]
    cnt_ref[...] = incl[tm - 1:tm, :]
    rec = jnp.zeros(logits.shape, F32)
    for slot, val in ((R_W1, w1), (R_W2, w2), (R_E1, i1.astype(F32)), (R_E2, i2.astype(F32)),
                      (R_RANK1, rank1), (R_RANK2, rank2)):
        rec = jnp.where(lane == slot, val, rec)
    r_ref[...] = rec


def _route(h32, w_router):
    n, d = h32.shape
    n_experts = w_router.shape[1]
    w_pad = jnp.pad(w_router, ((0, 0), (0, LANES - n_experts)))
    tm = _row_tile(n, 512)
    return pl.pallas_call(
        functools.partial(_route_kernel, n_experts=n_experts),
        grid=(n // tm,),
        in_specs=[pl.BlockSpec((tm, d), lambda i: (i, 0)), _const_spec(w_pad.shape)],
        out_specs=[pl.BlockSpec((tm, LANES), lambda i: (i, 0)), _const_spec((1, LANES))],
        out_shape=[jax.ShapeDtypeStruct((n, LANES), F32), jax.ShapeDtypeStruct((1, LANES), F32)],
        scratch_shapes=[pltpu.VMEM((1, LANES), F32)],
        compiler_params=_params("arbitrary"),
        name="route",
    )(h32, w_pad)


def _for_each_row(n_rows, fn):
    assert n_rows % DMA_UNROLL == 0, n_rows

    def trip(i, carry):
        for j in range(DMA_UNROLL):
            fn(i * DMA_UNROLL + j)
        return carry

    lax.fori_loop(0, n_rows // DMA_UNROLL, trip, 0)


def _dispatch_kernel(pos_ref, h_ref, xin_ref, xg_ref, sem):
    del xin_ref
    tm = pos_ref.shape[1]

    def row_copy(r, s):
        return pltpu.make_async_copy(h_ref.at[pl.ds(r, 1)], xg_ref.at[pl.ds(pos_ref[s, r], 1)], sem)

    _for_each_row(tm, lambda r: [row_copy(r, s).start() for s in range(TOP_K)])
    _for_each_row(tm, lambda r: [row_copy(r, s).wait() for s in range(TOP_K)])


def _dispatch(pos, h32, n_rows):
    n, d = h32.shape
    tm = _row_tile(n, 1024)
    return pl.pallas_call(
        _dispatch_kernel,
        grid=(n // tm,),
        in_specs=[pl.BlockSpec((TOP_K, tm), lambda i: (0, i), memory_space=pltpu.SMEM),
                  pl.BlockSpec((tm, d), lambda i: (i, 0)), pl.BlockSpec(memory_space=pl.ANY)],
        out_specs=pl.BlockSpec(memory_space=pl.ANY),
        out_shape=jax.ShapeDtypeStruct((n_rows, d), F32),
        scratch_shapes=[pltpu.SemaphoreType.DMA(())],
        input_output_aliases={2: 0},
        compiler_params=_params("arbitrary"),
        name="moe_dispatch",
    )(pos, h32, jnp.zeros((n_rows, d), F32))


def _group_ffn_kernel(te_ref, nt_ref, x_ref, wg_ref, wu_ref, wd_ref, y_ref, acc_scr, x16_scr):
    del te_ref
    j = pl.program_id(1)
    last = pl.num_programs(1) - 1
    live = pl.program_id(0) < nt_ref[0]

    @pl.when(live & (j == 0))
    def _():
        x16_scr[...] = x_ref[...].astype(BF16)
        acc_scr[...] = jnp.zeros_like(acc_scr)

    @pl.when(live)
    def _():
        x = x16_scr[...]
        a = _silu(_dot(x, wg_ref[...])) * _dot(x, wu_ref[...])
        acc_scr[...] += _dot(a.astype(BF16), wd_ref[...])

    @pl.when(live & (j == last))
    def _():
        y_ref[...] = acc_scr[...]

    @pl.when(jnp.logical_not(live) & (j == last))
    def _():
        y_ref[...] = jnp.zeros_like(y_ref)


def _group_ffn(tile_expert, n_tiles, xg, wg, wu, wd, tm):
    n_rows, d = xg.shape
    f = wg.shape[2]
    tf = _row_tile(f, 512)
    nf = f // tf

    def col(i, j, te, nt):
        return jnp.where(i < nt[0], j, nf - 1)

    grid_spec = pltpu.PrefetchScalarGridSpec(
        num_scalar_prefetch=2,
        grid=(n_rows // tm, nf),
        in_specs=[pl.BlockSpec((tm, d), lambda i, j, te, nt: (i, 0)),
                  pl.BlockSpec((None, d, tf), lambda i, j, te, nt: (te[i], 0, col(i, j, te, nt))),
                  pl.BlockSpec((None, d, tf), lambda i, j, te, nt: (te[i], 0, col(i, j, te, nt))),
                  pl.BlockSpec((None, tf, d), lambda i, j, te, nt: (te[i], col(i, j, te, nt), 0))],
        out_specs=pl.BlockSpec((tm, d), lambda i, j, te, nt: (i, 0)),
        scratch_shapes=[pltpu.VMEM((tm, d), F32), pltpu.VMEM((tm, d), BF16)],
    )
    return pl.pallas_call(
        _group_ffn_kernel,
        grid_spec=grid_spec,
        out_shape=jax.ShapeDtypeStruct((n_rows, d), F32),
        compiler_params=_params("arbitrary", "arbitrary"),
        name="moe_ffn",
    )(tile_expert, n_tiles, xg, wg, wu, wd)


def _combine_kernel(pos_ref, r_ref, h32_ref, y_ref, lg_ref, lb_ref, o32_ref, o16_ref, g_scr, sem, *, alpha):
    tm = h32_ref.shape[0]

    def row_copy(r, s):
        return pltpu.make_async_copy(y_ref.at[pl.ds(pos_ref[s, r], 1)], g_scr.at[s, pl.ds(r, 1)], sem)

    _for_each_row(tm, lambda r: [row_copy(r, s).start() for s in range(TOP_K)])
    _for_each_row(tm, lambda r: [row_copy(r, s).wait() for s in range(TOP_K)])
    rec = r_ref[...]
    f = rec[:, R_W1:R_W1 + 1] * g_scr[0] + rec[:, R_W2:R_W2 + 1] * g_scr[1]
    y = _layer_norm(alpha * h32_ref[...] + f, lg_ref[...], lb_ref[...])
    o32_ref[...] = y
    o16_ref[...] = y.astype(BF16)


def _combine(pos, rec, h32, yg, lg, lb, alpha):
    n, d = h32.shape
    tm = _row_tile(n, 256)
    row = pl.BlockSpec((tm, d), lambda i: (i, 0))
    return pl.pallas_call(
        functools.partial(_combine_kernel, alpha=alpha),
        grid=(n // tm,),
        in_specs=[pl.BlockSpec((TOP_K, tm), lambda i: (0, i), memory_space=pltpu.SMEM),
                  pl.BlockSpec((tm, LANES), lambda i: (i, 0)), row,
                  pl.BlockSpec(memory_space=pl.ANY), _const_spec(lg.shape), _const_spec(lb.shape)],
        out_specs=[row, row],
        out_shape=[jax.ShapeDtypeStruct((n, d), F32), jax.ShapeDtypeStruct((n, d), BF16)],
        scratch_shapes=[pltpu.VMEM((TOP_K, tm, d), F32), pltpu.SemaphoreType.DMA(())],
        compiler_params=_params("arbitrary"),
        name="moe_combine",
    )(pos, rec, h32, yg, lg, lb)


def _moe(h32, w_router, wg, wu, wd, lg, lb, alpha):
    n, d = h32.shape
    n_experts = wg.shape[0]
    tm = 1024 if n >= 8192 else 256
    max_tiles = (TOP_K * n) // tm + n_experts
    rec, counts = _route(h32, w_router)

    cnt = counts[0, :n_experts].astype(jnp.int32)
    ends = jnp.cumsum((cnt + tm - 1) // tm * tm)
    starts = ends - (cnt + tm - 1) // tm * tm
    expert = rec[:, R_E1:R_E2 + 1].astype(jnp.int32)
    rank = rec[:, R_RANK1:R_RANK2 + 1].astype(jnp.int32)
    pos = (starts[expert] + rank).T
    n_tiles = (ends[-1] // tm).astype(jnp.int32)
    tile_start = jnp.arange(max_tiles, dtype=jnp.int32) * tm
    tile_expert = jnp.minimum(jnp.searchsorted(ends, tile_start, side='right'), n_experts - 1).astype(jnp.int32)
    tile_expert = jnp.where(jnp.arange(max_tiles) < n_tiles, tile_expert, tile_expert[jnp.maximum(n_tiles - 1, 0)])

    xg = _dispatch(pos, h32, max_tiles * tm)
    yg = _group_ffn(tile_expert, n_tiles.reshape(1), xg, wg, wu, wd, tm)
    return _combine(pos, rec, h32, yg, lg, lb, alpha)


def _layer_weights(w, l, d, mix_w):
    rank = w['w_a2'].shape[1]
    qk_w = w['w_a2'].shape[2]
    sizes = (qk_w, qk_w, mix_w, mix_w, rank, mix_w, mix_w, mix_w, FOX_HEADS, mix_w, mix_w, 3 * d)
    offs = [0]
    for s in sizes:
        offs.append(offs[-1] + s)
    w_in = w['w_in'][l]
    col = lambda a, b: w_in[:, offs[a]:offs[b]]
    pad_cols = lambda x: jnp.pad(x, ((0, 0), (0, LANES - x.shape[1])))
    return dict(
        gla_wz=col(0, 4).astype(BF16),
        gla_wa1=pad_cols(col(4, 5)).astype(BF16),
        gla_wa2=jnp.pad(w['w_a2'][l], ((0, LANES - rank), (0, 0))).astype(BF16),
        gla_ba=w['b_a'][l].reshape(1, -1),
        gla_gn=w['gla_norm_g'][l].reshape(1, -1),
        fox_w=col(5, 8).astype(BF16),
        fox_wf=pad_cols(col(8, 9)).astype(BF16),
        fox_bf=pad_cols(w['b_f'][l].reshape(1, -1)),
        gm_w=col(9, 11).astype(BF16),
        gm_g=w['gmlp_norm_g'][l].reshape(1, -1),
        gm_b=w['gmlp_norm_b'][l].reshape(1, -1),
        gate_w=col(11, 12).astype(BF16),
        branch_w=w['w_branch'][l].astype(BF16),
        out_w=w['w_out'][l].astype(BF16),
    )


def _trunk(x, w, gla_s, fox_k, fox_v, fox_lf, keep_gmlp_rows):
    bsz, seq, d = x.shape
    depth = w['w_in'].shape[0]
    mix_w = w['gla_norm_g'].shape[1]
    alpha = (2 * depth) ** 0.25
    n = bsz * seq
    dk = w['w_a2'].shape[2] // GLA_HEADS
    dv = mix_w // GLA_HEADS
    dh = mix_w // FOX_HEADS
    gm_chunk = GMLP_CHUNK if seq % GMLP_CHUNK == 0 else seq

    h32, h16 = _ln_in(x.reshape(n, d), w['ln_in_g'], w['ln_in_b'])
    s_out, gm_out = [], []
    fox_out = [jnp.zeros((depth, bsz, seq, FOX_HEADS, dh), F32), jnp.zeros((depth, bsz, seq, FOX_HEADS, dh), F32),
               jnp.zeros((depth, bsz, seq, FOX_HEADS), F32)]
    for l in range(depth):
        lw = _layer_weights(w, l, d, mix_w)
        h16_seq = h16.reshape(bsz, seq, d)

        if gla_s is None:
            s0_t = jnp.zeros((bsz, GLA_HEADS, dv, dk), F32)
        else:
            s0_t = jnp.swapaxes(gla_s[l], -1, -2)
        o_gla, s_t = _gla(h16_seq, lw['gla_wz'], lw['gla_wa1'], lw['gla_wa2'], lw['gla_ba'], lw['gla_gn'], s0_t)
        s_out.append(jnp.swapaxes(s_t, -1, -2))

        gm = _gmlp(h16_seq, lw['gm_w'], lw['gm_g'], lw['gm_b'], w['gmlp_ws'][l][:, :gm_chunk, :gm_chunk],
                   w['gmlp_bs'][l][:, :gm_chunk].T, keep_gmlp_rows)
        o_gm = gm[0]
        if keep_gmlp_rows:
            gm_out.append(gm[1])

        if fox_k is None:
            d0 = jnp.zeros((bsz, 1, LANES), F32)
        else:
            past = fox_k.shape[2]
            kp, vp_t, d0 = _fox_cache(fox_k, fox_v, fox_lf, l)
        *fox_out, q_t, kn, vn_t = _fox_proj(h16_seq, lw['fox_w'], lw['fox_wf'], lw['fox_bf'], d0, l, fox_out)
        if fox_k is None:
            kp, vp_t = kn, vn_t
        o_fox = _fox_attn(q_t, kp, vp_t, kn, vn_t, dh=dh)

        h32, h16 = _merge(h32, h16, o_gla.reshape(n, mix_w), o_gm.reshape(n, mix_w), o_fox.reshape(n, mix_w),
                          lw['gate_w'], lw['branch_w'], lw['out_w'],
                          w['ln_g'][l, 0].reshape(1, d), w['ln_b'][l, 0].reshape(1, d), alpha)
        lg, lb = w['ln_g'][l, 1].reshape(1, d), w['ln_b'][l, 1].reshape(1, d)
        j = l // 2
        if l % 2 == 0:
            h32, h16 = _ffn(h32, h16, w['ffn_w_gate'][j].astype(BF16), w['ffn_w_up'][j].astype(BF16),
                            w['ffn_w_down'][j].astype(BF16), lg, lb, alpha)
        else:
            h32, h16 = _moe(h32, w['moe_router'][j], w['moe_w_gate'][j].astype(BF16),
                            w['moe_w_up'][j].astype(BF16), w['moe_w_down'][j].astype(BF16), lg, lb, alpha)
    gm_stack = jnp.stack(gm_out) if keep_gmlp_rows else None
    return (h32.reshape(bsz, seq, d), jnp.stack(s_out), *fox_out, gm_stack)


def kernel(x_prompt, x_sample, state_gla, cache_fox_k, cache_fox_v, cache_fox_logf, ln_in_g, ln_in_b, w_in, w_a2, b_a, gla_norm_g, b_f, gmlp_norm_g, gmlp_norm_b, gmlp_ws, gmlp_bs, w_branch, w_out, ln_g, ln_b, ffn_w_gate, ffn_w_up, ffn_w_down, moe_router, moe_w_gate, moe_w_up, moe_w_down):
    w = dict(ln_in_g=ln_in_g, ln_in_b=ln_in_b, w_in=w_in, w_a2=w_a2, b_a=b_a, gla_norm_g=gla_norm_g,
             b_f=b_f, gmlp_norm_g=gmlp_norm_g, gmlp_norm_b=gmlp_norm_b, gmlp_ws=gmlp_ws, gmlp_bs=gmlp_bs,
             w_branch=w_branch, w_out=w_out, ln_g=ln_g, ln_b=ln_b, ffn_w_gate=ffn_w_gate,
             ffn_w_up=ffn_w_up, ffn_w_down=ffn_w_down, moe_router=moe_router, moe_w_gate=moe_w_gate,
             moe_w_up=moe_w_up, moe_w_down=moe_w_down)
    y_p, gla_p, fk_p, fv_p, flf_p, _ = _trunk(x_prompt, w, None, None, None, None, False)
    y_s, gla_s, fk_s, fv_s, flf_s, gmv_s = _trunk(x_sample, w, state_gla, cache_fox_k, cache_fox_v,
                                                  cache_fox_logf, True)
    return (y_p, y_s, gla_p, fk_p, fv_p, flf_p, gla_s, fk_s, fv_s, flf_s, gmv_s)
```

```python
import functools

import jax
import jax.numpy as jnp
from jax import lax
from jax.experimental import pallas as pl
from jax.experimental.pallas import tpu as pltpu

F32 = jnp.float32
BF16 = jnp.bfloat16

GLA_HEADS = 4
GLA_CHUNK = 64
GLA_TAU = 16.0
FOX_HEADS = 8
GMLP_GROUPS = 4
GMLP_CHUNK = 128
TOP_K = 2
ATTN_GROUPS = (8, 4, 2, 1)
DMA_UNROLL = 8
LN_EPS = 1e-5
RMS_EPS = 1e-6

V7X_VMEM_BYTES = 64 * 1024 * 1024
VMEM_LIMIT_BYTES = V7X_VMEM_BYTES * 3 // 4
LANES = 128


def _params(*semantics, flags=None):
    return pltpu.CompilerParams(dimension_semantics=semantics, vmem_limit_bytes=VMEM_LIMIT_BYTES, flags=flags)


def _const_spec(shape):
    zeros = (0,) * len(shape)
    return pl.BlockSpec(shape, lambda *_: zeros)


def _dot(a, b):
    return jnp.dot(a, b, preferred_element_type=F32)


def _dot_nt(a, b):
    return lax.dot_general(a, b, (((1,), (1,)), ((), ())), preferred_element_type=F32)


def _dot_tn(a, b):
    return lax.dot_general(a, b, (((0,), (0,)), ((), ())), preferred_element_type=F32)


def _layer_norm(x, g, b):
    mu = jnp.mean(x, axis=-1, keepdims=True)
    xc = x - mu
    var = jnp.mean(xc * xc, axis=-1, keepdims=True)
    return xc * lax.rsqrt(var + LN_EPS) * g + b


def _log_sigmoid(x):
    return -(jnp.maximum(-x, 0.0) + jnp.log1p(jnp.exp(-jnp.abs(x))))


def _silu(x):
    return x * jax.nn.sigmoid(x)


def _split3(x):
    hi = x.astype(BF16)
    r1 = x - hi.astype(F32)
    mid = r1.astype(BF16)
    lo = (r1 - mid.astype(F32)).astype(BF16)
    return hi, mid, lo


def _tri_cumsum(tri_bf, x):
    hi, mid, lo = _split3(x)
    return _dot(tri_bf, hi) + _dot(tri_bf, mid) + _dot(tri_bf, lo)


def _lower_tri(n):
    row = lax.broadcasted_iota(jnp.int32, (n, n), 0)
    col = lax.broadcasted_iota(jnp.int32, (n, n), 1)
    return row >= col


def _row_tile(n, cap):
    t = min(n, cap)
    assert n % t == 0, (n, cap)
    return t


def _ln_in_kernel(x_ref, g_ref, b_ref, o32_ref, o16_ref):
    y = _layer_norm(x_ref[...], g_ref[...], b_ref[...])
    o32_ref[...] = y
    o16_ref[...] = y.astype(BF16)


def _ln_in(x2, g, b):
    n, d = x2.shape
    tm = _row_tile(n, 512)
    row = pl.BlockSpec((tm, d), lambda i: (i, 0))
    return pl.pallas_call(
        _ln_in_kernel,
        grid=(n // tm,),
        in_specs=[row, _const_spec((1, d)), _const_spec((1, d))],
        out_specs=[row, row],
        out_shape=[jax.ShapeDtypeStruct((n, d), F32), jax.ShapeDtypeStruct((n, d), BF16)],
        compiler_params=_params("parallel"),
        name="ln_in",
    )(x2, g.reshape(1, d), b.reshape(1, d))


def _gla_kernel(h_ref, wz_ref, wa1_ref, wa2_ref, ba_ref, gn_ref, s0_ref, o_ref, sfin_ref, s_scr,
                *, chunk, n_chunk, dk, dv):
    t = pl.program_id(1)

    @pl.when(t == 0)
    def _():
        s_scr[...] = s0_ref[...]

    qk_w = GLA_HEADS * dk
    v_w = GLA_HEADS * dv
    x = h_ref[...]
    z = _dot(x, wz_ref[...])
    a1 = _dot(x, wa1_ref[...])
    log_a = _log_sigmoid(_dot(a1.astype(BF16), wa2_ref[...]) + ba_ref[...]) * (1.0 / GLA_TAU)

    tri = _lower_tri(chunk)
    tri_bf = jnp.where(tri, 1.0, 0.0).astype(BF16)
    for c in range(n_chunk):
        rows = slice(c * chunk, (c + 1) * chunk)
        b = _tri_cumsum(tri_bf, log_a[rows])
        b_last = b[chunk - 1:chunk, :]
        q = z[rows, 0:qk_w]
        k = z[rows, qk_w:2 * qk_w]
        v = z[rows, 2 * qk_w:2 * qk_w + v_w]
        r = z[rows, 2 * qk_w + v_w:2 * qk_w + 2 * v_w]
        qe = (q * (dk ** -0.5)) * jnp.exp(b)
        ke = k * jnp.exp(-b)
        kd = k * jnp.exp(b_last - b)
        decay = jnp.exp(b_last)
        for hd in range(GLA_HEADS):
            kl = slice(hd * dk, (hd + 1) * dk)
            vl = slice(hd * dv, (hd + 1) * dv)
            qe_h = qe[:, kl].astype(BF16)
            v_h = v[:, vl].astype(BF16)
            a = jnp.where(tri, _dot_nt(qe_h, ke[:, kl].astype(BF16)), 0.0)
            s_t = s_scr[hd]
            o = _dot(a.astype(BF16), v_h) + _dot_nt(qe_h, s_t.astype(BF16))
            s_scr[hd] = s_t * decay[:, kl] + _dot_tn(v_h, kd[:, kl].astype(BF16))
            o = o * lax.rsqrt(jnp.mean(o * o, axis=-1, keepdims=True) + RMS_EPS) * gn_ref[:, vl]
            o_ref[rows, vl] = (o * _silu(r[:, vl])).astype(o_ref.dtype)

    @pl.when(t == pl.num_programs(1) - 1)
    def _():
        sfin_ref[...] = s_scr[...]


def _gla(h16, wz, wa1, wa2, ba, gn, s0_t):
    bsz, seq, d = h16.shape
    _, heads, dv, dk = s0_t.shape
    chunk = GLA_CHUNK if seq % GLA_CHUNK == 0 else seq
    tc = _row_tile(seq, 512)
    kern = functools.partial(_gla_kernel, chunk=chunk, n_chunk=tc // chunk, dk=dk, dv=dv)
    state = pl.BlockSpec((None, heads, dv, dk), lambda b, t: (b, 0, 0, 0))
    return pl.pallas_call(
        kern,
        grid=(bsz, seq // tc),
        in_specs=[pl.BlockSpec((None, tc, d), lambda b, t: (b, t, 0)),
                  _const_spec(wz.shape), _const_spec(wa1.shape), _const_spec(wa2.shape),
                  _const_spec(ba.shape), _const_spec(gn.shape), state],
        out_specs=[pl.BlockSpec((None, tc, heads * dv), lambda b, t: (b, t, 0)), state],
        out_shape=[jax.ShapeDtypeStruct((bsz, seq, heads * dv), BF16),
                   jax.ShapeDtypeStruct(s0_t.shape, F32)],
        scratch_shapes=[pltpu.VMEM((heads, dv, dk), F32)],
        compiler_params=_params("parallel", "arbitrary"),
        name="gla",
    )(h16, wz, wa1, wa2, ba, gn, s0_t)


def _gmlp_kernel(h_ref, w_ref, g_ref, b_ref, ws_ref, bs_ref, o_ref, *v_ref, chunk, n_chunk, width):
    x = h_ref[...]
    z = _dot(x, w_ref[...])
    u = jax.nn.gelu(z[:, :width])
    v = _layer_norm(jax.nn.gelu(z[:, width:]), g_ref[...], b_ref[...])
    if v_ref:
        v_ref[0][...] = v
    gdim = width // GMLP_GROUPS
    tri = _lower_tri(chunk)
    for g in range(GMLP_GROUPS):
        w_g = jnp.where(tri, ws_ref[g], 0.0).astype(BF16)
        bias = bs_ref[:, g:g + 1]
        cols = slice(g * gdim, (g + 1) * gdim)
        for n in range(n_chunk):
            rows = slice(n * chunk, (n + 1) * chunk)
            mixed = _dot(w_g, v[rows, cols].astype(BF16)) + bias
            o_ref[rows, cols] = (u[rows, cols] * mixed).astype(o_ref.dtype)


def _gmlp(h16, w, g, b, ws, bs_t, keep_v):
    bsz, seq, d = h16.shape
    width = g.shape[-1]
    chunk = ws.shape[-1]
    tc = _row_tile(seq, 512)
    kern = functools.partial(_gmlp_kernel, chunk=chunk, n_chunk=tc // chunk, width=width)
    tile = pl.BlockSpec((None, tc, width), lambda i, t: (i, t, 0))
    out_specs = [tile]
    out_shape = [jax.ShapeDtypeStruct((bsz, seq, width), BF16)]
    if keep_v:
        out_specs.append(tile)
        out_shape.append(jax.ShapeDtypeStruct((bsz, seq, width), F32))
    return pl.pallas_call(
        kern,
        grid=(bsz, seq // tc),
        in_specs=[pl.BlockSpec((None, tc, d), lambda i, t: (i, t, 0)),
                  _const_spec(w.shape), _const_spec(g.shape), _const_spec(b.shape),
                  _const_spec(ws.shape), _const_spec(bs_t.shape)],
        out_specs=out_specs,
        out_shape=out_shape,
        compiler_params=_params("parallel", "parallel"),
        name="gmlp",
    )(h16, w, g, b, ws, bs_t)


def _transpose_bf16(x):
    r, c = x.shape
    if r % LANES == 0 and c % LANES == 0:
        return x.T
    row = lax.broadcasted_iota(jnp.int32, (r, r), 0)
    col = lax.broadcasted_iota(jnp.int32, (r, r), 1)
    return _dot_tn(x.astype(BF16), jnp.where(row == col, 1.0, 0.0).astype(BF16))


def _fox_pack(dcum, zq, zk, zv, q_ref, k_ref, v_ref, *, dh):
    tm = dcum.shape[0]
    hi = dcum.astype(BF16).astype(F32)
    r1 = dcum - hi
    mid = r1.astype(BF16).astype(F32)
    lo = (r1 - mid).astype(BF16).astype(F32)
    lane = lax.broadcasted_iota(jnp.int32, (tm, LANES), 1)
    sub = lax.broadcasted_iota(jnp.int32, (8, tm), 0)
    pad = jnp.zeros((LANES - dh - 8, tm), F32)
    ones_row = jnp.where(sub == 0, 1.0, 0.0)
    zv_t = _transpose_bf16(zv)
    if q_ref is not None:
        zq_t = _transpose_bf16(zq * (dh ** -0.5))
        hi_t, mid_t, lo_t = _transpose_bf16(hi), _transpose_bf16(mid), _transpose_bf16(lo)
    for h in range(FOX_HEADS):
        cols = slice((h // 2) * LANES, (h // 2 + 1) * LANES)
        odd = h % 2
        data = (lane >= dh) if odd else (lane < dh)
        slot = lane - (0 if odd else dh)
        d_hi = jnp.broadcast_to(hi[:, h:h + 1], (tm, LANES))
        d_mid = jnp.broadcast_to(mid[:, h:h + 1], (tm, LANES))
        d_lo = jnp.broadcast_to(lo[:, h:h + 1], (tm, LANES))
        ones_first = jnp.where(slot < 0, 0.0, jnp.where(slot < 3, 1.0, 0.0))
        k_extra = jnp.where(slot == 3, -d_hi, jnp.where(slot == 4, -d_mid, jnp.where(slot == 5, -d_lo, ones_first)))
        k_ref[h] = jnp.where(data, zk[:, cols], k_extra).astype(k_ref.dtype)
        rows = slice(h * dh, (h + 1) * dh)
        v_ref[h] = jnp.concatenate([zv_t[rows], ones_row, pad], axis=0).astype(v_ref.dtype)
        if q_ref is not None:
            bias = jnp.where(sub == 0, jnp.broadcast_to(hi_t[h:h + 1], (8, tm)),
                             jnp.where(sub == 1, jnp.broadcast_to(mid_t[h:h + 1], (8, tm)),
                                       jnp.where(sub == 2, jnp.broadcast_to(lo_t[h:h + 1], (8, tm)),
                                                 jnp.where(sub < 6, 1.0, 0.0))))
            parts = [bias, pad, zq_t[rows]] if odd else [zq_t[rows], bias, pad]
            q_ref[h] = jnp.concatenate(parts, axis=0).astype(q_ref.dtype)


def _fox_proj_kernel(h_ref, w_ref, wf_ref, bf_ref, d0_ref, *refs, width):
    kout_ref, vout_ref, lf_ref, q_ref, k_ref, v_ref, d_scr = refs[-7:]
    t = pl.program_id(1)

    @pl.when(t == 0)
    def _():
        d_scr[...] = d0_ref[...]

    tm = h_ref.shape[0]
    x = h_ref[...]
    z = _dot(x, w_ref[...])
    zq, zk, zv = z[:, :width], z[:, width:2 * width], z[:, 2 * width:]
    kout_ref[...] = zk.reshape(kout_ref.shape)
    vout_ref[...] = zv.reshape(vout_ref.shape)
    lf = _log_sigmoid(_dot(x, wf_ref[...]) + bf_ref[...])
    lane = lax.broadcasted_iota(jnp.int32, lf.shape, 1)
    lf = jnp.where(lane < FOX_HEADS, lf, 0.0)
    lf_ref[...] = lf[:, :FOX_HEADS]
    tri_bf = jnp.where(_lower_tri(tm), 1.0, 0.0).astype(BF16)
    dcum = _tri_cumsum(tri_bf, lf) + d_scr[...]
    d_scr[...] = dcum[tm - 1:tm, :]
    _fox_pack(dcum, zq, zk, zv, q_ref, k_ref, v_ref, dh=width // FOX_HEADS)


def _fox_proj(h16, w, wf, bf, d0, layer, stacked):
    bsz, seq, d = h16.shape
    width = w.shape[1] // 3
    dh = width // FOX_HEADS
    tm = _row_tile(seq, 512)
    kern = functools.partial(_fox_proj_kernel, width=width)
    heads = pl.BlockSpec((None, None, tm, FOX_HEADS, dh), lambda b, t: (layer, b, t, 0, 0))
    rows = pl.BlockSpec((None, FOX_HEADS, tm, LANES), lambda b, t: (b, 0, t, 0))
    cols = pl.BlockSpec((None, FOX_HEADS, LANES, tm), lambda b, t: (b, 0, 0, t))
    rows_shape = jax.ShapeDtypeStruct((bsz, FOX_HEADS, seq, LANES), BF16)
    cols_shape = jax.ShapeDtypeStruct((bsz, FOX_HEADS, LANES, seq), BF16)
    in_specs = [pl.BlockSpec((None, tm, d), lambda b, t: (b, t, 0)),
                _const_spec(w.shape), _const_spec(wf.shape), _const_spec(bf.shape),
                pl.BlockSpec((None, 1, LANES), lambda b, t: (b, 0, 0))]
    args = [h16, w, wf, bf, d0]
    aliases = {len(args) + i: i for i in range(len(stacked))}
    in_specs += [pl.BlockSpec(memory_space=pl.ANY)] * len(stacked)
    args += list(stacked)
    return pl.pallas_call(
        kern,
        grid=(bsz, seq // tm),
        in_specs=in_specs,
        out_specs=[heads, heads, pl.BlockSpec((None, None, tm, FOX_HEADS), lambda b, t: (layer, b, t, 0)),
                   cols, rows, cols],
        out_shape=[jax.ShapeDtypeStruct(s.shape, s.dtype) for s in stacked] + [cols_shape, rows_shape, cols_shape],
        scratch_shapes=[pltpu.VMEM((1, LANES), F32)],
        input_output_aliases=aliases,
        compiler_params=_params("parallel", "arbitrary"),
        name="fox_proj",
    )(*args)


def _fox_cumlog_kernel(lf_ref, d_ref, dend_ref):
    x = lf_ref[...]
    heads, past = x.shape
    lane = lax.broadcasted_iota(jnp.int32, x.shape, 1)
    shift = 1
    while shift < past:
        x = x + jnp.where(lane >= shift, pltpu.roll(x, shift, 1), 0.0)
        shift *= 2
    d_ref[...] = x
    total = jnp.broadcast_to(x[:, past - 1:past], (heads, LANES))
    sub = lax.broadcasted_iota(jnp.int32, (heads, LANES), 0)
    col = lax.broadcasted_iota(jnp.int32, (heads, LANES), 1)
    dend_ref[...] = jnp.sum(jnp.where(sub == col, total, 0.0), axis=0, keepdims=True)


def _fox_cumlog(lf_t, layer):
    _, bsz, heads, past = lf_t.shape
    return pl.pallas_call(
        _fox_cumlog_kernel,
        grid=(bsz,),
        in_specs=[pl.BlockSpec((None, None, heads, past), lambda b: (layer, b, 0, 0))],
        out_specs=[pl.BlockSpec((None, heads, past), lambda b: (b, 0, 0)),
                   pl.BlockSpec((None, 1, LANES), lambda b: (b, 0, 0))],
        out_shape=[jax.ShapeDtypeStruct((bsz, heads, past), F32), jax.ShapeDtypeStruct((bsz, 1, LANES), F32)],
        compiler_params=_params("parallel"),
        name="fox_cumlog",
    )(lf_t)


def _fox_decode_kernel(q_ref, kt_ref, vt_ref, d_ref, kn_ref, vn_ref, o_ref, *, dh):
    j = pl.program_id(1)
    tq = q_ref.shape[2]
    past = kt_ref.shape[2]
    sub = lax.broadcasted_iota(jnp.int32, (8, past), 0)
    pad = jnp.zeros((LANES - dh - 8, past), F32)
    ones_blk = jnp.where(sub == 0, 1.0, 0.0)
    causal = lax.broadcasted_iota(jnp.int32, (tq, tq), 1) <= lax.broadcasted_iota(jnp.int32, (tq, tq), 0)
    lane = lax.broadcasted_iota(jnp.int32, (tq, LANES), 1)
    outs = []
    for hh in range(2):
        odd = hh == 1
        d = d_ref[pl.ds(2 * j + hh, 1), :]
        hi = d.astype(BF16).astype(F32)
        r1 = d - hi
        mid = r1.astype(BF16).astype(F32)
        lo = (r1 - mid).astype(BF16).astype(F32)
        bias = jnp.where(sub == 3, -hi, jnp.where(sub == 4, -mid, jnp.where(sub == 5, -lo,
                         jnp.where(sub < 3, 1.0, 0.0))))
        kt, vt = kt_ref[hh], vt_ref[hh]
        k_ext = jnp.concatenate([bias, pad, kt] if odd else [kt, bias, pad], axis=0).astype(BF16)
        v_ext = jnp.concatenate([ones_blk, pad, vt] if odd else [vt, ones_blk, pad], axis=0).astype(BF16)
        qt = q_ref[hh]
        s_c = _dot_tn(qt, k_ext)
        kn_t = _transpose_bf16(kn_ref[hh].astype(F32)).astype(BF16)
        s_n = jnp.where(causal, _dot_tn(qt, kn_t), -jnp.inf)
        m = jnp.maximum(jnp.max(s_c, axis=-1, keepdims=True), jnp.max(s_n, axis=-1, keepdims=True))
        vn = vn_ref[hh].astype(F32)
        if odd:
            vn = jnp.concatenate([vn[dh:dh + 8], vn[dh + 8:], vn[:dh]], axis=0)
        acc = (_dot_nt(jnp.exp(s_c - m).astype(BF16), v_ext)
               + _dot_nt(jnp.exp(s_n - m).astype(BF16), vn.astype(BF16)))
        outs.append(acc / (acc[:, 0:1] if odd else acc[:, dh:dh + 1]))
    o_ref[...] = jnp.where(lane < dh, outs[0], outs[1]).astype(o_ref.dtype)


def _fox_decode(q_t, kt, vt, dcum, kn, vn_t, layer, *, dh):
    bsz, heads, _, seq = q_t.shape
    past = kt.shape[-1]
    new_t = pl.BlockSpec((None, 2, LANES, seq), lambda b, j: (b, j, 0, 0))
    cache = pl.BlockSpec((None, None, 2, dh, past), lambda b, j: (layer, b, j, 0, 0))
    return pl.pallas_call(
        functools.partial(_fox_decode_kernel, dh=dh),
        grid=(bsz, heads // 2),
        in_specs=[new_t, cache, cache, pl.BlockSpec((None, heads, past), lambda b, j: (b, 0, 0)),
                  pl.BlockSpec((None, 2, seq, LANES), lambda b, j: (b, j, 0, 0)), new_t],
        out_specs=pl.BlockSpec((None, seq, LANES), lambda b, j: (b, 0, j)),
        out_shape=jax.ShapeDtypeStruct((bsz, seq, heads // 2 * LANES), BF16),
        compiler_params=_params("parallel", "parallel"),
        name="fox_decode",
    )(q_t, kt, vt, dcum, kn, vn_t)


def _fox_attn_kernel(q_ref, kp_ref, vp_ref, kn_ref, vn_ref, o_ref, s_scr, sd_scr, acc_scr,
                     *, tk, dh, full_base, full_per_q):
    tq = q_ref.shape[2]
    n_full = full_base + pl.program_id(2) * full_per_q
    q = [q_ref[hh] for hh in range(2)]
    causal = lax.broadcasted_iota(jnp.int32, (tq, tq), 0) <= lax.broadcasted_iota(jnp.int32, (tq, tq), 1)

    def scores(first, width):
        def step(i, ms):
            rows = pl.ds(pl.multiple_of(first + i * width, tk), width)
            out = []
            for hh in range(2):
                s = _dot(kp_ref[hh, rows, :], q[hh])
                s_scr[hh, rows, :] = s
                out.append(jnp.maximum(ms[hh], jnp.max(s, axis=0, keepdims=True)))
            return tuple(out)
        return step

    def accumulate(first, width):
        def step(i, carry):
            rows = pl.ds(pl.multiple_of(first + i * width, tk), width)
            for hh in range(2):
                p = jnp.exp(s_scr[hh, rows, :] - ms[hh]).astype(BF16)
                acc_scr[hh] += _dot(vp_ref[hh, :, rows], p)
            return carry
        return step

    runs, first, left = [], 0, n_full
    for group in ATTN_GROUPS:
        count = left // group
        runs.append((first, group * tk, count))
        first = first + count * group * tk
        left = left - count * group

    ms = tuple(jnp.full((1, tq), -jnp.inf, F32) for _ in range(2))
    for first, width, count in runs:
        ms = lax.fori_loop(0, count, scores(first, width), ms)
    ms = list(ms)
    for hh in range(2):
        s = jnp.where(causal, _dot(kn_ref[hh], q[hh]), -jnp.inf)
        sd_scr[hh] = s
        ms[hh] = jnp.maximum(ms[hh], jnp.max(s, axis=0, keepdims=True))
        acc_scr[hh] = _dot(vn_ref[hh], jnp.exp(sd_scr[hh] - ms[hh]).astype(BF16))
    for first, width, count in runs:
        lax.fori_loop(0, count, accumulate(first, width), 0)
    halves = []
    for hh in range(2):
        acc = acc_scr[hh]
        halves.append(acc[:dh] / acc[dh:dh + 1])
    o_ref[...] = _transpose_bf16(jnp.concatenate(halves, axis=0)).astype(o_ref.dtype)


def _fox_attn(q_t, kp, vp_t, kn, vn_t, *, dh):
    bsz, heads, _, seq = q_t.shape
    self_attn = kp is kn
    past = kp.shape[2]
    off = 0 if self_attn else past
    tq = _row_tile(seq, 2 * LANES)
    nq = seq // tq
    tk = tq if self_attn else _row_tile(past, 512)
    assert off % tk == 0 and (nq == 1 or tq % tk == 0), (off, tq, tk)
    full_base, full_per_q = off // tk, tq // tk
    max_full = max(full_base + (nq - 1) * full_per_q, 1)
    kern = functools.partial(_fox_attn_kernel, tk=tk, dh=dh, full_base=full_base, full_per_q=full_per_q)
    return pl.pallas_call(
        kern,
        grid=(bsz, heads // 2, nq),
        in_specs=[pl.BlockSpec((None, 2, LANES, tq), lambda b, j, i: (b, j, 0, i)),
                  pl.BlockSpec((None, 2, past, LANES), lambda b, j, i: (b, j, 0, 0)),
                  pl.BlockSpec((None, 2, LANES, past), lambda b, j, i: (b, j, 0, 0)),
                  pl.BlockSpec((None, 2, tq, LANES), lambda b, j, i: (b, j, i, 0)),
                  pl.BlockSpec((None, 2, LANES, tq), lambda b, j, i: (b, j, 0, i))],
        out_specs=pl.BlockSpec((None, tq, LANES), lambda b, j, i: (b, i, j)),
        out_shape=jax.ShapeDtypeStruct((bsz, seq, heads // 2 * LANES), BF16),
        scratch_shapes=[pltpu.VMEM((2, max_full * tk, tq), F32), pltpu.VMEM((2, tq, tq), F32),
                        pltpu.VMEM((2, LANES, tq), F32)],
        compiler_params=_params("parallel", "parallel", "arbitrary"),
        name="fox_attn",
    )(q_t, kp, vp_t, kn, vn_t)


def _merge_kernel(h32_ref, h16_ref, og_ref, om_ref, of_ref, wg_ref, wb_ref, wo_ref, lg_ref, lb_ref,
                  o32_ref, o16_ref, *, alpha):
    d = h32_ref.shape[1]
    x = h16_ref[...]
    m = None
    for r, o_ref in enumerate((og_ref, om_ref, of_ref)):
        gate = jax.nn.sigmoid(_dot(x, wg_ref[:, r * d:(r + 1) * d]))
        term = gate * _dot(o_ref[...], wb_ref[r])
        m = term if m is None else m + term
    y = alpha * h32_ref[...] + _dot(m.astype(BF16), wo_ref[...])
    y = _layer_norm(y, lg_ref[...], lb_ref[...])
    o32_ref[...] = y
    o16_ref[...] = y.astype(BF16)


def _merge(h32, h16, o_gla, o_gm, o_fox, wg, wb, wo, lg, lb, alpha):
    n, d = h32.shape
    w = o_gla.shape[1]
    tm = _row_tile(n, 256)
    row = pl.BlockSpec((tm, d), lambda i: (i, 0))
    mix = pl.BlockSpec((tm, w), lambda i: (i, 0))
    return pl.pallas_call(
        functools.partial(_merge_kernel, alpha=alpha),
        grid=(n // tm,),
        in_specs=[row, row, mix, mix, mix, _const_spec(wg.shape), _const_spec(wb.shape),
                  _const_spec(wo.shape), _const_spec(lg.shape), _const_spec(lb.shape)],
        out_specs=[row, row],
        out_shape=[jax.ShapeDtypeStruct((n, d), F32), jax.ShapeDtypeStruct((n, d), BF16)],
        compiler_params=_params("parallel"),
        name="merge",
    )(h32, h16, o_gla, o_gm, o_fox, wg, wb, wo, lg, lb)


def _ffn_kernel(h32_ref, h16_ref, wg_ref, wu_ref, wd_ref, lg_ref, lb_ref, o32_ref, o16_ref, acc_scr, *, alpha):
    j = pl.program_id(1)

    @pl.when(j == 0)
    def _():
        acc_scr[...] = jnp.zeros_like(acc_scr)

    x = h16_ref[...]
    a = _silu(_dot(x, wg_ref[...])) * _dot(x, wu_ref[...])
    acc_scr[...] += _dot(a.astype(BF16), wd_ref[...])

    @pl.when(j == pl.num_programs(1) - 1)
    def _():
        y = _layer_norm(alpha * h32_ref[...] + acc_scr[...], lg_ref[...], lb_ref[...])
        o32_ref[...] = y
        o16_ref[...] = y.astype(BF16)


def _ffn(h32, h16, wg, wu, wd, lg, lb, alpha):
    n, d = h32.shape
    f = wg.shape[1]
    tm = _row_tile(n, 1024)
    tf = _row_tile(f, 512)
    row = pl.BlockSpec((tm, d), lambda i, j: (i, 0))
    return pl.pallas_call(
        functools.partial(_ffn_kernel, alpha=alpha),
        grid=(n // tm, f // tf),
        in_specs=[row, row, pl.BlockSpec((d, tf), lambda i, j: (0, j)), pl.BlockSpec((d, tf), lambda i, j: (0, j)),
                  pl.BlockSpec((tf, d), lambda i, j: (j, 0)), _const_spec(lg.shape), _const_spec(lb.shape)],
        out_specs=[row, row],
        out_shape=[jax.ShapeDtypeStruct((n, d), F32), jax.ShapeDtypeStruct((n, d), BF16)],
        scratch_shapes=[pltpu.VMEM((tm, d), F32)],
        compiler_params=_params("parallel", "arbitrary"),
        name="ffn",
    )(h32, h16, wg, wu, wd, lg, lb)


R_W1, R_W2, R_E1, R_E2, R_RANK1, R_RANK2 = range(6)


def _route_kernel(h_ref, w_ref, r_ref, cnt_ref, cnt_scr, *, n_experts):
    @pl.when(pl.program_id(0) == 0)
    def _():
        cnt_scr[...] = jnp.zeros_like(cnt_scr)

    tm = h_ref.shape[0]
    logits = jnp.dot(h_ref[...], w_ref[...], preferred_element_type=F32, precision=lax.Precision.HIGHEST)
    lane = lax.broadcasted_iota(jnp.int32, logits.shape, 1)
    logits = jnp.where(lane < n_experts, logits, -jnp.inf)
    v1 = jnp.max(logits, axis=-1, keepdims=True)
    i1 = jnp.min(jnp.where(logits == v1, lane, LANES), axis=-1, keepdims=True)
    rest = jnp.where(lane == i1, -jnp.inf, logits)
    v2 = jnp.max(rest, axis=-1, keepdims=True)
    i2 = jnp.min(jnp.where(rest == v2, lane, LANES), axis=-1, keepdims=True)
    e2 = jnp.exp(v2 - v1)
    w1 = 1.0 / (1.0 + e2)
    w2 = e2 / (1.0 + e2)
    hot1 = lane == i1
    hot2 = lane == i2
    sel = jnp.where(hot1, 1.0, jnp.where(hot2, 1.0, 0.0))
    tri_bf = jnp.where(_lower_tri(tm), 1.0, 0.0).astype(BF16)
    incl = _dot(tri_bf, sel.astype(BF16)) + cnt_scr[...]
    excl = incl - sel
    rank1 = jnp.sum(jnp.where(hot1, excl, 0.0), axis=-1, keepdims=True)
    rank2 = jnp.sum(jnp.where(hot2, excl, 0.0), axis=-1, keepdims=True)
    cnt_scr[...] = incl[tm - 1:tm, :]
    cnt_ref[...] = incl[tm - 1:tm, :]
    rec = jnp.zeros(logits.shape, F32)
    for slot, val in ((R_W1, w1), (R_W2, w2), (R_E1, i1.astype(F32)), (R_E2, i2.astype(F32)),
                      (R_RANK1, rank1), (R_RANK2, rank2)):
        rec = jnp.where(lane == slot, val, rec)
    r_ref[...] = rec


def _route(h32, w_router):
    n, d = h32.shape
    n_experts = w_router.shape[1]
    w_pad = jnp.pad(w_router, ((0, 0), (0, LANES - n_experts)))
    tm = _row_tile(n, 512)
    return pl.pallas_call(
        functools.partial(_route_kernel, n_experts=n_experts),
        grid=(n // tm,),
        in_specs=[pl.BlockSpec((tm, d), lambda i: (i, 0)), _const_spec(w_pad.shape)],
        out_specs=[pl.BlockSpec((tm, LANES), lambda i: (i, 0)), _const_spec((1, LANES))],
        out_shape=[jax.ShapeDtypeStruct((n, LANES), F32), jax.ShapeDtypeStruct((1, LANES), F32)],
        scratch_shapes=[pltpu.VMEM((1, LANES), F32)],
        compiler_params=_params("arbitrary"),
        name="route",
    )(h32, w_pad)


def _for_each_row(n_rows, fn):
    assert n_rows % DMA_UNROLL == 0, n_rows

    def trip(i, carry):
        for j in range(DMA_UNROLL):
            fn(i * DMA_UNROLL + j)
        return carry

    lax.fori_loop(0, n_rows // DMA_UNROLL, trip, 0)


def _dispatch_kernel(pos_ref, h_ref, xin_ref, xg_ref, sem):
    del xin_ref
    tm = pos_ref.shape[1]

    def row_copy(r, s):
        return pltpu.make_async_copy(h_ref.at[pl.ds(r, 1)], xg_ref.at[pl.ds(pos_ref[s, r], 1)], sem)

    _for_each_row(tm, lambda r: [row_copy(r, s).start() for s in range(TOP_K)])
    _for_each_row(tm, lambda r: [row_copy(r, s).wait() for s in range(TOP_K)])


def _dispatch(pos, h32, n_rows):
    n, d = h32.shape
    tm = _row_tile(n, 1024)
    return pl.pallas_call(
        _dispatch_kernel,
        grid=(n // tm,),
        in_specs=[pl.BlockSpec((TOP_K, tm), lambda i: (0, i), memory_space=pltpu.SMEM),
                  pl.BlockSpec((tm, d), lambda i: (i, 0)), pl.BlockSpec(memory_space=pl.ANY)],
        out_specs=pl.BlockSpec(memory_space=pl.ANY),
        out_shape=jax.ShapeDtypeStruct((n_rows, d), F32),
        scratch_shapes=[pltpu.SemaphoreType.DMA(())],
        input_output_aliases={2: 0},
        compiler_params=_params("arbitrary"),
        name="moe_dispatch",
    )(pos, h32, jnp.zeros((n_rows, d), F32))


def _group_ffn_kernel(te_ref, nt_ref, x_ref, wg_ref, wu_ref, wd_ref, y_ref, acc_scr, x16_scr):
    del te_ref
    j = pl.program_id(1)
    last = pl.num_programs(1) - 1
    live = pl.program_id(0) < nt_ref[0]

    @pl.when(live & (j == 0))
    def _():
        x16_scr[...] = x_ref[...].astype(BF16)
        acc_scr[...] = jnp.zeros_like(acc_scr)

    @pl.when(live)
    def _():
        x = x16_scr[...]
        a = _silu(_dot(x, wg_ref[...])) * _dot(x, wu_ref[...])
        acc_scr[...] += _dot(a.astype(BF16), wd_ref[...])

    @pl.when(live & (j == last))
    def _():
        y_ref[...] = acc_scr[...]

    @pl.when(jnp.logical_not(live) & (j == last))
    def _():
        y_ref[...] = jnp.zeros_like(y_ref)


def _group_ffn(tile_expert, n_tiles, xg, wg, wu, wd, tm):
    n_rows, d = xg.shape
    f = wg.shape[2]
    tf = _row_tile(f, 512)
    nf = f // tf

    def col(i, j, te, nt):
        return jnp.where(i < nt[0], j, nf - 1)

    grid_spec = pltpu.PrefetchScalarGridSpec(
        num_scalar_prefetch=2,
        grid=(n_rows // tm, nf),
        in_specs=[pl.BlockSpec((tm, d), lambda i, j, te, nt: (i, 0)),
                  pl.BlockSpec((None, d, tf), lambda i, j, te, nt: (te[i], 0, col(i, j, te, nt))),
                  pl.BlockSpec((None, d, tf), lambda i, j, te, nt: (te[i], 0, col(i, j, te, nt))),
                  pl.BlockSpec((None, tf, d), lambda i, j, te, nt: (te[i], col(i, j, te, nt), 0))],
        out_specs=pl.BlockSpec((tm, d), lambda i, j, te, nt: (i, 0)),
        scratch_shapes=[pltpu.VMEM((tm, d), F32), pltpu.VMEM((tm, d), BF16)],
    )
    return pl.pallas_call(
        _group_ffn_kernel,
        grid_spec=grid_spec,
        out_shape=jax.ShapeDtypeStruct((n_rows, d), F32),
        compiler_params=_params("arbitrary", "arbitrary"),
        name="moe_ffn",
    )(tile_expert, n_tiles, xg, wg, wu, wd)


def _combine_kernel(pos_ref, r_ref, h32_ref, y_ref, lg_ref, lb_ref, o32_ref, o16_ref, g_scr, sem, *, alpha):
    tm = h32_ref.shape[0]

    def row_copy(r, s):
        return pltpu.make_async_copy(y_ref.at[pl.ds(pos_ref[s, r], 1)], g_scr.at[s, pl.ds(r, 1)], sem)

    _for_each_row(tm, lambda r: [row_copy(r, s).start() for s in range(TOP_K)])
    _for_each_row(tm, lambda r: [row_copy(r, s).wait() for s in range(TOP_K)])
    rec = r_ref[...]
    f = rec[:, R_W1:R_W1 + 1] * g_scr[0] + rec[:, R_W2:R_W2 + 1] * g_scr[1]
    y = _layer_norm(alpha * h32_ref[...] + f, lg_ref[...], lb_ref[...])
    o32_ref[...] = y
    o16_ref[...] = y.astype(BF16)


def _combine(pos, rec, h32, yg, lg, lb, alpha):
    n, d = h32.shape
    tm = _row_tile(n, 256)
    row = pl.BlockSpec((tm, d), lambda i: (i, 0))
    return pl.pallas_call(
        functools.partial(_combine_kernel, alpha=alpha),
        grid=(n // tm,),
        in_specs=[pl.BlockSpec((TOP_K, tm), lambda i: (0, i), memory_space=pltpu.SMEM),
                  pl.BlockSpec((tm, LANES), lambda i: (i, 0)), row,
                  pl.BlockSpec(memory_space=pl.ANY), _const_spec(lg.shape), _const_spec(lb.shape)],
        out_specs=[row, row],
        out_shape=[jax.ShapeDtypeStruct((n, d), F32), jax.ShapeDtypeStruct((n, d), BF16)],
        scratch_shapes=[pltpu.VMEM((TOP_K, tm, d), F32), pltpu.SemaphoreType.DMA(())],
        compiler_params=_params("arbitrary"),
        name="moe_combine",
    )(pos, rec, h32, yg, lg, lb)


def _moe(h32, w_router, wg, wu, wd, lg, lb, alpha):
    n, d = h32.shape
    n_experts = wg.shape[0]
    tm = 1024 if n >= 8192 else 256
    max_tiles = (TOP_K * n) // tm + n_experts
    rec, counts = _route(h32, w_router)

    cnt = counts[0, :n_experts].astype(jnp.int32)
    ends = jnp.cumsum((cnt + tm - 1) // tm * tm)
    starts = ends - (cnt + tm - 1) // tm * tm
    expert = rec[:, R_E1:R_E2 + 1].astype(jnp.int32)
    rank = rec[:, R_RANK1:R_RANK2 + 1].astype(jnp.int32)
    pos = (starts[expert] + rank).T
    n_tiles = (ends[-1] // tm).astype(jnp.int32)
    tile_start = jnp.arange(max_tiles, dtype=jnp.int32) * tm
    tile_expert = jnp.minimum(jnp.searchsorted(ends, tile_start, side='right'), n_experts - 1).astype(jnp.int32)
    tile_expert = jnp.where(jnp.arange(max_tiles) < n_tiles, tile_expert, tile_expert[jnp.maximum(n_tiles - 1, 0)])

    xg = _dispatch(pos, h32, max_tiles * tm)
    yg = _group_ffn(tile_expert, n_tiles.reshape(1), xg, wg, wu, wd, tm)
    return _combine(pos, rec, h32, yg, lg, lb, alpha)


def _layer_weights(w, l, d, mix_w):
    rank = w['w_a2'].shape[1]
    qk_w = w['w_a2'].shape[2]
    sizes = (qk_w, qk_w, mix_w, mix_w, rank, mix_w, mix_w, mix_w, FOX_HEADS, mix_w, mix_w, 3 * d)
    offs = [0]
    for s in sizes:
        offs.append(offs[-1] + s)
    w_in = w['w_in'][l]
    col = lambda a, b: w_in[:, offs[a]:offs[b]]
    pad_cols = lambda x: jnp.pad(x, ((0, 0), (0, LANES - x.shape[1])))
    return dict(
        gla_wz=col(0, 4).astype(BF16),
        gla_wa1=pad_cols(col(4, 5)).astype(BF16),
        gla_wa2=jnp.pad(w['w_a2'][l], ((0, LANES - rank), (0, 0))).astype(BF16),
        gla_ba=w['b_a'][l].reshape(1, -1),
        gla_gn=w['gla_norm_g'][l].reshape(1, -1),
        fox_w=col(5, 8).astype(BF16),
        fox_wf=pad_cols(col(8, 9)).astype(BF16),
        fox_bf=pad_cols(w['b_f'][l].reshape(1, -1)),
        gm_w=col(9, 11).astype(BF16),
        gm_g=w['gmlp_norm_g'][l].reshape(1, -1),
        gm_b=w['gmlp_norm_b'][l].reshape(1, -1),
        gate_w=col(11, 12).astype(BF16),
        branch_w=w['w_branch'][l].astype(BF16),
        out_w=w['w_out'][l].astype(BF16),
    )


def _trunk(x, w, gla_s, fox_k, fox_v, fox_lf, keep_gmlp_rows):
    bsz, seq, d = x.shape
    depth = w['w_in'].shape[0]
    mix_w = w['gla_norm_g'].shape[1]
    alpha = (2 * depth) ** 0.25
    n = bsz * seq
    dk = w['w_a2'].shape[2] // GLA_HEADS
    dv = mix_w // GLA_HEADS
    dh = mix_w // FOX_HEADS
    gm_chunk = GMLP_CHUNK if seq % GMLP_CHUNK == 0 else seq

    h32, h16 = _ln_in(x.reshape(n, d), w['ln_in_g'], w['ln_in_b'])
    if fox_k is not None:
        cache_kt = jnp.transpose(fox_k, (0, 1, 3, 4, 2))
        cache_vt = jnp.transpose(fox_v, (0, 1, 3, 4, 2))
        cache_lf_t = jnp.transpose(fox_lf, (0, 1, 3, 2))
    s_out, gm_out = [], []
    fox_out = [jnp.zeros((depth, bsz, seq, FOX_HEADS, dh), F32), jnp.zeros((depth, bsz, seq, FOX_HEADS, dh), F32),
               jnp.zeros((depth, bsz, seq, FOX_HEADS), F32)]
    for l in range(depth):
        lw = _layer_weights(w, l, d, mix_w)
        h16_seq = h16.reshape(bsz, seq, d)

        if gla_s is None:
            s0_t = jnp.zeros((bsz, GLA_HEADS, dv, dk), F32)
        else:
            s0_t = jnp.swapaxes(gla_s[l], -1, -2)
        o_gla, s_t = _gla(h16_seq, lw['gla_wz'], lw['gla_wa1'], lw['gla_wa2'], lw['gla_ba'], lw['gla_gn'], s0_t)
        s_out.append(jnp.swapaxes(s_t, -1, -2))

        gm = _gmlp(h16_seq, lw['gm_w'], lw['gm_g'], lw['gm_b'], w['gmlp_ws'][l][:, :gm_chunk, :gm_chunk],
                   w['gmlp_bs'][l][:, :gm_chunk].T, keep_gmlp_rows)
        o_gm = gm[0]
        if keep_gmlp_rows:
            gm_out.append(gm[1])

        if fox_k is None:
            d0 = jnp.zeros((bsz, 1, LANES), F32)
        else:
            dcum, d0 = _fox_cumlog(cache_lf_t, l)
        *fox_out, q_t, kn, vn_t = _fox_proj(h16_seq, lw['fox_w'], lw['fox_wf'], lw['fox_bf'], d0, l, fox_out)
        if fox_k is None:
            o_fox = _fox_attn(q_t, kn, vn_t, kn, vn_t, dh=dh)
        else:
            o_fox = _fox_decode(q_t, cache_kt, cache_vt, dcum, kn, vn_t, l, dh=dh)

        h32, h16 = _merge(h32, h16, o_gla.reshape(n, mix_w), o_gm.reshape(n, mix_w), o_fox.reshape(n, mix_w),
                          lw['gate_w'], lw['branch_w'], lw['out_w'],
                          w['ln_g'][l, 0].reshape(1, d), w['ln_b'][l, 0].reshape(1, d), alpha)
        lg, lb = w['ln_g'][l, 1].reshape(1, d), w['ln_b'][l, 1].reshape(1, d)
        j = l // 2
        if l % 2 == 0:
            h32, h16 = _ffn(h32, h16, w['ffn_w_gate'][j].astype(BF16), w['ffn_w_up'][j].astype(BF16),
                            w['ffn_w_down'][j].astype(BF16), lg, lb, alpha)
        else:
            h32, h16 = _moe(h32, w['moe_router'][j], w['moe_w_gate'][j].astype(BF16),
                            w['moe_w_up'][j].astype(BF16), w['moe_w_down'][j].astype(BF16), lg, lb, alpha)
    gm_stack = jnp.stack(gm_out) if keep_gmlp_rows else None
    return (h32.reshape(bsz, seq, d), jnp.stack(s_out), *fox_out, gm_stack)


def kernel(x_prompt, x_sample, state_gla, cache_fox_k, cache_fox_v, cache_fox_logf, ln_in_g, ln_in_b, w_in, w_a2, b_a, gla_norm_g, b_f, gmlp_norm_g, gmlp_norm_b, gmlp_ws, gmlp_bs, w_branch, w_out, ln_g, ln_b, ffn_w_gate, ffn_w_up, ffn_w_down, moe_router, moe_w_gate, moe_w_up, moe_w_down):
    w = dict(ln_in_g=ln_in_g, ln_in_b=ln_in_b, w_in=w_in, w_a2=w_a2, b_a=b_a, gla_norm_g=gla_norm_g,
             b_f=b_f, gmlp_norm_g=gmlp_norm_g, gmlp_norm_b=gmlp_norm_b, gmlp_ws=gmlp_ws, gmlp_bs=gmlp_bs,
             w_branch=w_branch, w_out=w_out, ln_g=ln_g, ln_b=ln_b, ffn_w_gate=ffn_w_gate,
             ffn_w_up=ffn_w_up, ffn_w_down=ffn_w_down, moe_router=moe_router, moe_w_gate=moe_w_gate,
             moe_w_up=moe_w_up, moe_w_down=moe_w_down)
    y_p, gla_p, fk_p, fv_p, flf_p, _ = _trunk(x_prompt, w, None, None, None, None, False)
    y_s, gla_s, fk_s, fv_s, flf_s, gmv_s = _trunk(x_sample, w, state_gla, cache_fox_k, cache_fox_v,
                                                  cache_fox_logf, True)
    return (y_p, y_s, gla_p, fk_p, fv_p, flf_p, gla_s, fk_s, fv_s, flf_s, gmv_s)
```

```python
import functools

import jax
import jax.numpy as jnp
from jax import lax
from jax.experimental import pallas as pl
from jax.experimental.pallas import tpu as pltpu

F32 = jnp.float32
BF16 = jnp.bfloat16

GLA_HEADS = 4
GLA_CHUNK = 64
GLA_TAU = 16.0
FOX_HEADS = 8
GMLP_GROUPS = 4
GMLP_CHUNK = 128
TOP_K = 2
ATTN_GROUPS = (16, 8, 4, 2, 1)
V_ROWS = 80
DMA_UNROLL = 8
LN_EPS = 1e-5
RMS_EPS = 1e-6

V7X_VMEM_BYTES = 64 * 1024 * 1024
VMEM_LIMIT_BYTES = V7X_VMEM_BYTES * 7 // 8
FF_TILE_CAP = 896
LANES = 128


def _params(*semantics, flags=None):
    return pltpu.CompilerParams(dimension_semantics=semantics, vmem_limit_bytes=VMEM_LIMIT_BYTES, flags=flags)


def _const_spec(shape):
    zeros = (0,) * len(shape)
    return pl.BlockSpec(shape, lambda *_: zeros)


def _dot(a, b):
    return jnp.dot(a, b, preferred_element_type=F32)


def _dot_nt(a, b):
    return lax.dot_general(a, b, (((1,), (1,)), ((), ())), preferred_element_type=F32)


def _dot_tn(a, b):
    return lax.dot_general(a, b, (((0,), (0,)), ((), ())), preferred_element_type=F32)


def _layer_norm(x, g, b):
    mu = jnp.mean(x, axis=-1, keepdims=True)
    xc = x - mu
    var = jnp.mean(xc * xc, axis=-1, keepdims=True)
    return xc * lax.rsqrt(var + LN_EPS) * g + b


def _log_sigmoid(x):
    return -(jnp.maximum(-x, 0.0) + jnp.log1p(jnp.exp(-jnp.abs(x))))


def _silu(x):
    return x * jax.nn.sigmoid(x)


def _split3(x):
    hi = x.astype(BF16)
    r1 = x - hi.astype(F32)
    mid = r1.astype(BF16)
    lo = (r1 - mid.astype(F32)).astype(BF16)
    return hi, mid, lo


def _tri_cumsum(tri_bf, x):
    hi, mid, lo = _split3(x)
    return _dot(tri_bf, hi) + _dot(tri_bf, mid) + _dot(tri_bf, lo)


def _lower_tri(n):
    row = lax.broadcasted_iota(jnp.int32, (n, n), 0)
    col = lax.broadcasted_iota(jnp.int32, (n, n), 1)
    return row >= col


def _row_tile(n, cap):
    t = min(n, cap)
    assert n % t == 0, (n, cap)
    return t


def _ff_tile(f):
    for t in range(FF_TILE_CAP, 0, -LANES):
        if f % t == 0:
            return t
    return f


def _ln_in_kernel(x_ref, g_ref, b_ref, o32_ref, o16_ref):
    y = _layer_norm(x_ref[...], g_ref[...], b_ref[...])
    o32_ref[...] = y
    o16_ref[...] = y.astype(BF16)


def _ln_in(x2, g, b):
    n, d = x2.shape
    tm = _row_tile(n, 512)
    row = pl.BlockSpec((tm, d), lambda i: (i, 0))
    return pl.pallas_call(
        _ln_in_kernel,
        grid=(n // tm,),
        in_specs=[row, _const_spec((1, d)), _const_spec((1, d))],
        out_specs=[row, row],
        out_shape=[jax.ShapeDtypeStruct((n, d), F32), jax.ShapeDtypeStruct((n, d), BF16)],
        compiler_params=_params("parallel"),
        name="ln_in",
    )(x2, g.reshape(1, d), b.reshape(1, d))


def _gla_kernel(h_ref, wz_ref, wa1_ref, wa2_ref, ba_ref, gn_ref, s0_ref, o_ref, sfin_ref, s_scr,
                *, chunk, n_chunk, dk, dv):
    t = pl.program_id(1)

    @pl.when(t == 0)
    def _():
        s_scr[...] = s0_ref[...]

    qk_w = GLA_HEADS * dk
    v_w = GLA_HEADS * dv
    x = h_ref[...]
    z = _dot(x, wz_ref[...])
    a1 = _dot(x, wa1_ref[...])
    log_a = _log_sigmoid(_dot(a1.astype(BF16), wa2_ref[...]) + ba_ref[...]) * (1.0 / GLA_TAU)

    tri = _lower_tri(chunk)
    tri_bf = jnp.where(tri, 1.0, 0.0).astype(BF16)
    for c in range(n_chunk):
        rows = slice(c * chunk, (c + 1) * chunk)
        b = _tri_cumsum(tri_bf, log_a[rows])
        b_last = b[chunk - 1:chunk, :]
        q = z[rows, 0:qk_w]
        k = z[rows, qk_w:2 * qk_w]
        v = z[rows, 2 * qk_w:2 * qk_w + v_w]
        r = z[rows, 2 * qk_w + v_w:2 * qk_w + 2 * v_w]
        qe = (q * (dk ** -0.5)) * jnp.exp(b)
        ke = k * jnp.exp(-b)
        kd = k * jnp.exp(b_last - b)
        decay = jnp.exp(b_last)
        for hd in range(GLA_HEADS):
            kl = slice(hd * dk, (hd + 1) * dk)
            vl = slice(hd * dv, (hd + 1) * dv)
            qe_h = qe[:, kl].astype(BF16)
            v_h = v[:, vl].astype(BF16)
            a = jnp.where(tri, _dot_nt(qe_h, ke[:, kl].astype(BF16)), 0.0)
            s_t = s_scr[hd]
            o = _dot(a.astype(BF16), v_h) + _dot_nt(qe_h, s_t.astype(BF16))
            s_scr[hd] = s_t * decay[:, kl] + _dot_tn(v_h, kd[:, kl].astype(BF16))
            o = o * lax.rsqrt(jnp.mean(o * o, axis=-1, keepdims=True) + RMS_EPS) * gn_ref[:, vl]
            o_ref[rows, vl] = (o * _silu(r[:, vl])).astype(o_ref.dtype)

    @pl.when(t == pl.num_programs(1) - 1)
    def _():
        sfin_ref[...] = s_scr[...]


def _gla(h16, wz, wa1, wa2, ba, gn, s0_t):
    bsz, seq, d = h16.shape
    _, heads, dv, dk = s0_t.shape
    chunk = GLA_CHUNK if seq % GLA_CHUNK == 0 else seq
    tc = _row_tile(seq, 512)
    kern = functools.partial(_gla_kernel, chunk=chunk, n_chunk=tc // chunk, dk=dk, dv=dv)
    state = pl.BlockSpec((None, heads, dv, dk), lambda b, t: (b, 0, 0, 0))
    return pl.pallas_call(
        kern,
        grid=(bsz, seq // tc),
        in_specs=[pl.BlockSpec((None, tc, d), lambda b, t: (b, t, 0)),
                  _const_spec(wz.shape), _const_spec(wa1.shape), _const_spec(wa2.shape),
                  _const_spec(ba.shape), _const_spec(gn.shape), state],
        out_specs=[pl.BlockSpec((None, tc, heads * dv), lambda b, t: (b, t, 0)), state],
        out_shape=[jax.ShapeDtypeStruct((bsz, seq, heads * dv), BF16),
                   jax.ShapeDtypeStruct(s0_t.shape, F32)],
        scratch_shapes=[pltpu.VMEM((heads, dv, dk), F32)],
        compiler_params=_params("parallel", "arbitrary"),
        name="gla",
    )(h16, wz, wa1, wa2, ba, gn, s0_t)


def _gmlp_kernel(h_ref, w_ref, g_ref, b_ref, ws_ref, bs_ref, o_ref, *v_ref, chunk, n_chunk, width):
    x = h_ref[...]
    z = _dot(x, w_ref[...])
    u = jax.nn.gelu(z[:, :width])
    v = _layer_norm(jax.nn.gelu(z[:, width:]), g_ref[...], b_ref[...])
    if v_ref:
        v_ref[0][...] = v
    gdim = width // GMLP_GROUPS
    tri = _lower_tri(chunk)
    for g in range(GMLP_GROUPS):
        w_g = jnp.where(tri, ws_ref[g], 0.0).astype(BF16)
        bias = bs_ref[:, g:g + 1]
        cols = slice(g * gdim, (g + 1) * gdim)
        for n in range(n_chunk):
            rows = slice(n * chunk, (n + 1) * chunk)
            mixed = _dot(w_g, v[rows, cols].astype(BF16)) + bias
            o_ref[rows, cols] = (u[rows, cols] * mixed).astype(o_ref.dtype)


def _gmlp(h16, w, g, b, ws, bs_t, keep_v):
    bsz, seq, d = h16.shape
    width = g.shape[-1]
    chunk = ws.shape[-1]
    tc = _row_tile(seq, 512)
    kern = functools.partial(_gmlp_kernel, chunk=chunk, n_chunk=tc // chunk, width=width)
    tile = pl.BlockSpec((None, tc, width), lambda i, t: (i, t, 0))
    out_specs = [tile]
    out_shape = [jax.ShapeDtypeStruct((bsz, seq, width), BF16)]
    if keep_v:
        out_specs.append(tile)
        out_shape.append(jax.ShapeDtypeStruct((bsz, seq, width), F32))
    return pl.pallas_call(
        kern,
        grid=(bsz, seq // tc),
        in_specs=[pl.BlockSpec((None, tc, d), lambda i, t: (i, t, 0)),
                  _const_spec(w.shape), _const_spec(g.shape), _const_spec(b.shape),
                  _const_spec(ws.shape), _const_spec(bs_t.shape)],
        out_specs=out_specs,
        out_shape=out_shape,
        compiler_params=_params("parallel", "parallel"),
        name="gmlp",
    )(h16, w, g, b, ws, bs_t)


def _transpose_bf16(x):
    r, c = x.shape
    if r % LANES == 0 and c % LANES == 0:
        return x.T
    row = lax.broadcasted_iota(jnp.int32, (r, r), 0)
    col = lax.broadcasted_iota(jnp.int32, (r, r), 1)
    return _dot_tn(x.astype(BF16), jnp.where(row == col, 1.0, 0.0).astype(BF16))


def _fox_pack(dcum, zq, zk, zv, q_ref, k_ref, v_ref, *, dh):
    tm = dcum.shape[0]
    hi = dcum.astype(BF16).astype(F32)
    r1 = dcum - hi
    mid = r1.astype(BF16).astype(F32)
    lo = (r1 - mid).astype(BF16).astype(F32)
    lane = lax.broadcasted_iota(jnp.int32, (tm, LANES), 1)
    sub = lax.broadcasted_iota(jnp.int32, (8, tm), 0)
    pad = jnp.zeros((LANES - dh - 8, tm), F32)
    ones_row = jnp.where(sub == 0, 1.0, 0.0)
    zv_t = _transpose_bf16(zv)
    if q_ref is not None:
        zq_t = _transpose_bf16(zq * (dh ** -0.5))
        hi_t, mid_t, lo_t = _transpose_bf16(hi), _transpose_bf16(mid), _transpose_bf16(lo)
    for h in range(FOX_HEADS):
        cols = slice((h // 2) * LANES, (h // 2 + 1) * LANES)
        odd = h % 2
        data = (lane >= dh) if odd else (lane < dh)
        slot = lane - (0 if odd else dh)
        d_hi = jnp.broadcast_to(hi[:, h:h + 1], (tm, LANES))
        d_mid = jnp.broadcast_to(mid[:, h:h + 1], (tm, LANES))
        d_lo = jnp.broadcast_to(lo[:, h:h + 1], (tm, LANES))
        ones_first = jnp.where(slot < 0, 0.0, jnp.where(slot < 3, 1.0, 0.0))
        k_extra = jnp.where(slot == 3, -d_hi, jnp.where(slot == 4, -d_mid, jnp.where(slot == 5, -d_lo, ones_first)))
        k_ref[h] = jnp.where(data, zk[:, cols], k_extra).astype(k_ref.dtype)
        rows = slice(h * dh, (h + 1) * dh)
        v_ref[h] = jnp.concatenate([zv_t[rows], ones_row, pad[:V_ROWS - dh - 8]], axis=0).astype(v_ref.dtype)
        if q_ref is not None:
            bias = jnp.where(sub == 0, jnp.broadcast_to(hi_t[h:h + 1], (8, tm)),
                             jnp.where(sub == 1, jnp.broadcast_to(mid_t[h:h + 1], (8, tm)),
                                       jnp.where(sub == 2, jnp.broadcast_to(lo_t[h:h + 1], (8, tm)),
                                                 jnp.where(sub < 6, 1.0, 0.0))))
            parts = [bias, pad, zq_t[rows]] if odd else [zq_t[rows], bias, pad]
            q_ref[h] = jnp.concatenate(parts, axis=0).astype(q_ref.dtype)


def _fox_proj_kernel(h_ref, w_ref, wf_ref, bf_ref, d0_ref, *refs, width):
    kout_ref, vout_ref, lf_ref, q_ref, k_ref, v_ref, d_scr = refs[-7:]
    t = pl.program_id(1)

    @pl.when(t == 0)
    def _():
        d_scr[...] = d0_ref[...]

    tm = h_ref.shape[0]
    x = h_ref[...]
    z = _dot(x, w_ref[...])
    zq, zk, zv = z[:, :width], z[:, width:2 * width], z[:, 2 * width:]
    kout_ref[...] = zk.reshape(kout_ref.shape)
    vout_ref[...] = zv.reshape(vout_ref.shape)
    lf = _log_sigmoid(_dot(x, wf_ref[...]) + bf_ref[...])
    lane = lax.broadcasted_iota(jnp.int32, lf.shape, 1)
    lf = jnp.where(lane < FOX_HEADS, lf, 0.0)
    lf_ref[...] = lf[:, :FOX_HEADS]
    tri_bf = jnp.where(_lower_tri(tm), 1.0, 0.0).astype(BF16)
    dcum = _tri_cumsum(tri_bf, lf) + d_scr[...]
    d_scr[...] = dcum[tm - 1:tm, :]
    _fox_pack(dcum, zq, zk, zv, q_ref, k_ref, v_ref, dh=width // FOX_HEADS)


def _fox_proj(h16, w, wf, bf, d0, layer, stacked):
    bsz, seq, d = h16.shape
    width = w.shape[1] // 3
    dh = width // FOX_HEADS
    tm = _row_tile(seq, 512)
    kern = functools.partial(_fox_proj_kernel, width=width)
    heads = pl.BlockSpec((None, None, tm, FOX_HEADS, dh), lambda b, t: (layer, b, t, 0, 0))
    rows = pl.BlockSpec((None, FOX_HEADS, tm, LANES), lambda b, t: (b, 0, t, 0))
    cols = pl.BlockSpec((None, FOX_HEADS, LANES, tm), lambda b, t: (b, 0, 0, t))
    rows_shape = jax.ShapeDtypeStruct((bsz, FOX_HEADS, seq, LANES), BF16)
    cols_shape = jax.ShapeDtypeStruct((bsz, FOX_HEADS, LANES, seq), BF16)
    v_cols = pl.BlockSpec((None, FOX_HEADS, V_ROWS, tm), lambda b, t: (b, 0, 0, t))
    v_cols_shape = jax.ShapeDtypeStruct((bsz, FOX_HEADS, V_ROWS, seq), BF16)
    in_specs = [pl.BlockSpec((None, tm, d), lambda b, t: (b, t, 0)),
                _const_spec(w.shape), _const_spec(wf.shape), _const_spec(bf.shape),
                pl.BlockSpec((None, 1, LANES), lambda b, t: (b, 0, 0))]
    args = [h16, w, wf, bf, d0]
    aliases = {len(args) + i: i for i in range(len(stacked))}
    in_specs += [pl.BlockSpec(memory_space=pl.ANY)] * len(stacked)
    args += list(stacked)
    return pl.pallas_call(
        kern,
        grid=(bsz, seq // tm),
        in_specs=in_specs,
        out_specs=[heads, heads, pl.BlockSpec((None, None, tm, FOX_HEADS), lambda b, t: (layer, b, t, 0)),
                   cols, rows, v_cols],
        out_shape=[jax.ShapeDtypeStruct(s.shape, s.dtype) for s in stacked] + [cols_shape, rows_shape, v_cols_shape],
        scratch_shapes=[pltpu.VMEM((1, LANES), F32)],
        input_output_aliases=aliases,
        compiler_params=_params("parallel", "arbitrary"),
        name="fox_proj",
    )(*args)


def _fox_cumlog_kernel(lf_ref, d_ref, dend_ref):
    x = lf_ref[...]
    heads, past = x.shape
    lane = lax.broadcasted_iota(jnp.int32, x.shape, 1)
    shift = 1
    while shift < past:
        x = x + jnp.where(lane >= shift, pltpu.roll(x, shift, 1), 0.0)
        shift *= 2
    d_ref[...] = x
    total = jnp.broadcast_to(x[:, past - 1:past], (heads, LANES))
    sub = lax.broadcasted_iota(jnp.int32, (heads, LANES), 0)
    col = lax.broadcasted_iota(jnp.int32, (heads, LANES), 1)
    dend_ref[...] = jnp.sum(jnp.where(sub == col, total, 0.0), axis=0, keepdims=True)


def _fox_cumlog(lf_t, layer):
    _, bsz, heads, past = lf_t.shape
    return pl.pallas_call(
        _fox_cumlog_kernel,
        grid=(bsz,),
        in_specs=[pl.BlockSpec((None, None, heads, past), lambda b: (layer, b, 0, 0))],
        out_specs=[pl.BlockSpec((None, heads, past), lambda b: (b, 0, 0)),
                   pl.BlockSpec((None, 1, LANES), lambda b: (b, 0, 0))],
        out_shape=[jax.ShapeDtypeStruct((bsz, heads, past), F32), jax.ShapeDtypeStruct((bsz, 1, LANES), F32)],
        compiler_params=_params("parallel"),
        name="fox_cumlog",
    )(lf_t)


def _fox_decode_kernel(q_ref, kt_ref, vt_ref, d_ref, kn_ref, vn_ref, o_ref, *, dh):
    j = pl.program_id(1)
    tq = q_ref.shape[2]
    past = kt_ref.shape[2]
    sub = lax.broadcasted_iota(jnp.int32, (8, past), 0)
    pad = jnp.zeros((LANES - dh - 8, past), F32)
    ones_blk = jnp.where(sub == 0, 1.0, 0.0)
    causal = lax.broadcasted_iota(jnp.int32, (tq, tq), 1) <= lax.broadcasted_iota(jnp.int32, (tq, tq), 0)
    lane = lax.broadcasted_iota(jnp.int32, (tq, LANES), 1)
    outs = []
    for hh in range(2):
        odd = hh == 1
        d = d_ref[pl.ds(2 * j + hh, 1), :]
        hi = d.astype(BF16).astype(F32)
        r1 = d - hi
        mid = r1.astype(BF16).astype(F32)
        lo = (r1 - mid).astype(BF16).astype(F32)
        bias = jnp.where(sub == 3, -hi, jnp.where(sub == 4, -mid, jnp.where(sub == 5, -lo,
                         jnp.where(sub < 3, 1.0, 0.0))))
        kt, vt = kt_ref[hh], vt_ref[hh]
        k_ext = jnp.concatenate([bias, pad, kt] if odd else [kt, bias, pad], axis=0).astype(BF16)
        v_ext = jnp.concatenate([ones_blk, pad, vt] if odd else [vt, ones_blk, pad], axis=0).astype(BF16)
        qt = q_ref[hh]
        s_c = _dot_tn(qt, k_ext)
        kn_t = _transpose_bf16(kn_ref[hh].astype(F32)).astype(BF16)
        s_n = jnp.where(causal, _dot_tn(qt, kn_t), -jnp.inf)
        m = jnp.maximum(jnp.max(s_c, axis=-1, keepdims=True), jnp.max(s_n, axis=-1, keepdims=True))
        vn = vn_ref[hh].astype(F32)
        fill = jnp.zeros((LANES - dh - 8, tq), F32)
        vn = jnp.concatenate([vn[dh:dh + 8], fill, vn[:dh]] if odd else [vn[:dh + 8], fill], axis=0)
        acc = (_dot_nt(jnp.exp(s_c - m).astype(BF16), v_ext)
               + _dot_nt(jnp.exp(s_n - m).astype(BF16), vn.astype(BF16)))
        outs.append(acc / (acc[:, 0:1] if odd else acc[:, dh:dh + 1]))
    o_ref[...] = jnp.where(lane < dh, outs[0], outs[1]).astype(o_ref.dtype)


def _fox_decode(q_t, kt, vt, dcum, kn, vn_t, layer, *, dh):
    bsz, heads, _, seq = q_t.shape
    past = kt.shape[-1]
    new_t = pl.BlockSpec((None, 2, LANES, seq), lambda b, j: (b, j, 0, 0))
    cache = pl.BlockSpec((None, None, 2, dh, past), lambda b, j: (layer, b, j, 0, 0))
    return pl.pallas_call(
        functools.partial(_fox_decode_kernel, dh=dh),
        grid=(bsz, heads // 2),
        in_specs=[new_t, cache, cache, pl.BlockSpec((None, heads, past), lambda b, j: (b, 0, 0)),
                  pl.BlockSpec((None, 2, seq, LANES), lambda b, j: (b, j, 0, 0)),
                  pl.BlockSpec((None, 2, V_ROWS, seq), lambda b, j: (b, j, 0, 0))],
        out_specs=pl.BlockSpec((None, seq, LANES), lambda b, j: (b, 0, j)),
        out_shape=jax.ShapeDtypeStruct((bsz, seq, heads // 2 * LANES), BF16),
        compiler_params=_params("parallel", "parallel"),
        name="fox_decode",
    )(q_t, kt, vt, dcum, kn, vn_t)


def _fox_attn_kernel(q_ref, kp_ref, vp_ref, kn_ref, vn_ref, o_ref, s_scr, sd_scr, acc_scr,
                     *, tk, dh, full_base, full_per_q):
    tq = q_ref.shape[2]
    n_full = full_base + pl.program_id(2) * full_per_q
    q = [q_ref[hh] for hh in range(2)]
    causal = lax.broadcasted_iota(jnp.int32, (tq, tq), 0) <= lax.broadcasted_iota(jnp.int32, (tq, tq), 1)

    def scores(first, width):
        def step(i, ms):
            rows = pl.ds(pl.multiple_of(first + i * width, tk), width)
            out = []
            for hh in range(2):
                s = _dot(kp_ref[hh, rows, :], q[hh])
                s_scr[hh, rows, :] = s
                out.append(jnp.maximum(ms[hh], jnp.max(s, axis=0, keepdims=True)))
            return tuple(out)
        return step

    def accumulate(first, width):
        def step(i, carry):
            rows = pl.ds(pl.multiple_of(first + i * width, tk), width)
            for hh in range(2):
                p = jnp.exp(s_scr[hh, rows, :] - ms[hh]).astype(BF16)
                acc_scr[hh] += _dot(vp_ref[hh, :, rows], p)
            return carry
        return step

    runs, first, left = [], 0, n_full
    for group in ATTN_GROUPS:
        count = left // group
        runs.append((first, group * tk, count))
        first = first + count * group * tk
        left = left - count * group

    ms = tuple(jnp.full((1, tq), -jnp.inf, F32) for _ in range(2))
    for first, width, count in runs:
        ms = lax.fori_loop(0, count, scores(first, width), ms)
    ms = list(ms)
    for hh in range(2):
        s = jnp.where(causal, _dot(kn_ref[hh], q[hh]), -jnp.inf)
        sd_scr[hh] = s
        ms[hh] = jnp.maximum(ms[hh], jnp.max(s, axis=0, keepdims=True))
        acc_scr[hh] = _dot(vn_ref[hh], jnp.exp(sd_scr[hh] - ms[hh]).astype(BF16))
    for first, width, count in runs:
        lax.fori_loop(0, count, accumulate(first, width), 0)
    halves = []
    for hh in range(2):
        acc = acc_scr[hh]
        halves.append(acc[:dh] / acc[dh:dh + 1])
    o_ref[...] = _transpose_bf16(jnp.concatenate(halves, axis=0)).astype(o_ref.dtype)


def _fox_attn(q_t, kp, vp_t, kn, vn_t, *, dh):
    bsz, heads, _, seq = q_t.shape
    self_attn = kp is kn
    past = kp.shape[2]
    off = 0 if self_attn else past
    tq = _row_tile(seq, 2 * LANES)
    nq = seq // tq
    tk = tq if self_attn else _row_tile(past, 512)
    assert off % tk == 0 and (nq == 1 or tq % tk == 0), (off, tq, tk)
    full_base, full_per_q = off // tk, tq // tk
    max_full = max(full_base + (nq - 1) * full_per_q, 1)
    kern = functools.partial(_fox_attn_kernel, tk=tk, dh=dh, full_base=full_base, full_per_q=full_per_q)
    return pl.pallas_call(
        kern,
        grid=(bsz, heads // 2, nq),
        in_specs=[pl.BlockSpec((None, 2, LANES, tq), lambda b, j, i: (b, j, 0, i)),
                  pl.BlockSpec((None, 2, past, LANES), lambda b, j, i: (b, j, 0, 0)),
                  pl.BlockSpec((None, 2, V_ROWS, past), lambda b, j, i: (b, j, 0, 0)),
                  pl.BlockSpec((None, 2, tq, LANES), lambda b, j, i: (b, j, i, 0)),
                  pl.BlockSpec((None, 2, V_ROWS, tq), lambda b, j, i: (b, j, 0, i))],
        out_specs=pl.BlockSpec((None, tq, LANES), lambda b, j, i: (b, i, j)),
        out_shape=jax.ShapeDtypeStruct((bsz, seq, heads // 2 * LANES), BF16),
        scratch_shapes=[pltpu.VMEM((2, max_full * tk, tq), F32), pltpu.VMEM((2, tq, tq), F32),
                        pltpu.VMEM((2, V_ROWS, tq), F32)],
        compiler_params=_params("parallel", "parallel", "arbitrary"),
        name="fox_attn",
    )(q_t, kp, vp_t, kn, vn_t)


def _merge_kernel(h32_ref, h16_ref, og_ref, om_ref, of_ref, wg_ref, wb_ref, wo_ref, lg_ref, lb_ref,
                  o32_ref, o16_ref, *, alpha):
    d = h32_ref.shape[1]
    x = h16_ref[...]
    m = None
    for r, o_ref in enumerate((og_ref, om_ref, of_ref)):
        gate = jax.nn.sigmoid(_dot(x, wg_ref[:, r * d:(r + 1) * d]))
        term = gate * _dot(o_ref[...], wb_ref[r])
        m = term if m is None else m + term
    y = alpha * h32_ref[...] + _dot(m.astype(BF16), wo_ref[...])
    y = _layer_norm(y, lg_ref[...], lb_ref[...])
    o32_ref[...] = y
    o16_ref[...] = y.astype(BF16)


def _merge(h32, h16, o_gla, o_gm, o_fox, wg, wb, wo, lg, lb, alpha):
    n, d = h32.shape
    w = o_gla.shape[1]
    tm = _row_tile(n, 256)
    row = pl.BlockSpec((tm, d), lambda i: (i, 0))
    mix = pl.BlockSpec((tm, w), lambda i: (i, 0))
    return pl.pallas_call(
        functools.partial(_merge_kernel, alpha=alpha),
        grid=(n // tm,),
        in_specs=[row, row, mix, mix, mix, _const_spec(wg.shape), _const_spec(wb.shape),
                  _const_spec(wo.shape), _const_spec(lg.shape), _const_spec(lb.shape)],
        out_specs=[row, row],
        out_shape=[jax.ShapeDtypeStruct((n, d), F32), jax.ShapeDtypeStruct((n, d), BF16)],
        compiler_params=_params("parallel"),
        name="merge",
    )(h32, h16, o_gla, o_gm, o_fox, wg, wb, wo, lg, lb)


def _ffn_kernel(h32_ref, h16_ref, wg_ref, wu_ref, wd_ref, lg_ref, lb_ref, o32_ref, o16_ref, acc_scr, *, alpha):
    j = pl.program_id(1)

    @pl.when(j == 0)
    def _():
        acc_scr[...] = jnp.zeros_like(acc_scr)

    x = h16_ref[...]
    a = _silu(_dot(x, wg_ref[...])) * _dot(x, wu_ref[...])
    acc_scr[...] += _dot(a.astype(BF16), wd_ref[...])

    @pl.when(j == pl.num_programs(1) - 1)
    def _():
        y = _layer_norm(alpha * h32_ref[...] + acc_scr[...], lg_ref[...], lb_ref[...])
        o32_ref[...] = y
        o16_ref[...] = y.astype(BF16)


def _ffn(h32, h16, wg, wu, wd, lg, lb, alpha):
    n, d = h32.shape
    f = wg.shape[1]
    tm = _row_tile(n, 1024)
    tf = _ff_tile(f)
    row = pl.BlockSpec((tm, d), lambda i, j: (i, 0))
    return pl.pallas_call(
        functools.partial(_ffn_kernel, alpha=alpha),
        grid=(n // tm, f // tf),
        in_specs=[row, row, pl.BlockSpec((d, tf), lambda i, j: (0, j)), pl.BlockSpec((d, tf), lambda i, j: (0, j)),
                  pl.BlockSpec((tf, d), lambda i, j: (j, 0)), _const_spec(lg.shape), _const_spec(lb.shape)],
        out_specs=[row, row],
        out_shape=[jax.ShapeDtypeStruct((n, d), F32), jax.ShapeDtypeStruct((n, d), BF16)],
        scratch_shapes=[pltpu.VMEM((tm, d), F32)],
        compiler_params=_params("parallel", "arbitrary"),
        name="ffn",
    )(h32, h16, wg, wu, wd, lg, lb)


R_W1, R_W2, R_E1, R_E2, R_RANK1, R_RANK2 = range(6)


def _route_kernel(h_ref, w_ref, r_ref, cnt_ref, cnt_scr, *, n_experts):
    @pl.when(pl.program_id(0) == 0)
    def _():
        cnt_scr[...] = jnp.zeros_like(cnt_scr)

    tm = h_ref.shape[0]
    logits = jnp.dot(h_ref[...], w_ref[...], preferred_element_type=F32, precision=lax.Precision.HIGHEST)
    lane = lax.broadcasted_iota(jnp.int32, logits.shape, 1)
    logits = jnp.where(lane < n_experts, logits, -jnp.inf)
    v1 = jnp.max(logits, axis=-1, keepdims=True)
    i1 = jnp.min(jnp.where(logits == v1, lane, LANES), axis=-1, keepdims=True)
    rest = jnp.where(lane == i1, -jnp.inf, logits)
    v2 = jnp.max(rest, axis=-1, keepdims=True)
    i2 = jnp.min(jnp.where(rest == v2, lane, LANES), axis=-1, keepdims=True)
    e2 = jnp.exp(v2 - v1)
    w1 = 1.0 / (1.0 + e2)
    w2 = e2 / (1.0 + e2)
    hot1 = lane == i1
    hot2 = lane == i2
    sel = jnp.where(hot1, 1.0, jnp.where(hot2, 1.0, 0.0))
    tri_bf = jnp.where(_lower_tri(tm), 1.0, 0.0).astype(BF16)
    incl = _dot(tri_bf, sel.astype(BF16)) + cnt_scr[...]
    excl = incl - sel
    rank1 = jnp.sum(jnp.where(hot1, excl, 0.0), axis=-1, keepdims=True)
    rank2 = jnp.sum(jnp.where(hot2, excl, 0.0), axis=-1, keepdims=True)
    cnt_scr[...] = incl[tm - 1:tm, :]
    cnt_ref[...] = incl[tm - 1:tm, :]
    rec = jnp.zeros(logits.shape, F32)
    for slot, val in ((R_W1, w1), (R_W2, w2), (R_E1, i1.astype(F32)), (R_E2, i2.astype(F32)),
                      (R_RANK1, rank1), (R_RANK2, rank2)):
        rec = jnp.where(lane == slot, val, rec)
    r_ref[...] = rec


def _route(h32, w_router):
    n, d = h32.shape
    n_experts = w_router.shape[1]
    w_pad = jnp.pad(w_router, ((0, 0), (0, LANES - n_experts)))
    tm = _row_tile(n, 512)
    return pl.pallas_call(
        functools.partial(_route_kernel, n_experts=n_experts),
        grid=(n // tm,),
        in_specs=[pl.BlockSpec((tm, d), lambda i: (i, 0)), _const_spec(w_pad.shape)],
        out_specs=[pl.BlockSpec((tm, LANES), lambda i: (i, 0)), _const_spec((1, LANES))],
        out_shape=[jax.ShapeDtypeStruct((n, LANES), F32), jax.ShapeDtypeStruct((1, LANES), F32)],
        scratch_shapes=[pltpu.VMEM((1, LANES), F32)],
        compiler_params=_params("arbitrary"),
        name="route",
    )(h32, w_pad)


def _for_each_row(n_rows, fn):
    assert n_rows % DMA_UNROLL == 0, n_rows

    def trip(i, carry):
        for j in range(DMA_UNROLL):
            fn(i * DMA_UNROLL + j)
        return carry

    lax.fori_loop(0, n_rows // DMA_UNROLL, trip, 0)


def _dispatch_kernel(pos_ref, h_ref, xin_ref, xg_ref, sem):
    del xin_ref
    tm = pos_ref.shape[1]

    def row_copy(r, s):
        return pltpu.make_async_copy(h_ref.at[pl.ds(r, 1)], xg_ref.at[pl.ds(pos_ref[s, r], 1)], sem)

    _for_each_row(tm, lambda r: [row_copy(r, s).start() for s in range(TOP_K)])
    _for_each_row(tm, lambda r: [row_copy(r, s).wait() for s in range(TOP_K)])


def _dispatch(pos, h32, n_rows):
    n, d = h32.shape
    tm = _row_tile(n, 1024)
    return pl.pallas_call(
        _dispatch_kernel,
        grid=(n // tm,),
        in_specs=[pl.BlockSpec((TOP_K, tm), lambda i: (0, i), memory_space=pltpu.SMEM),
                  pl.BlockSpec((tm, d), lambda i: (i, 0)), pl.BlockSpec(memory_space=pl.ANY)],
        out_specs=pl.BlockSpec(memory_space=pl.ANY),
        out_shape=jax.ShapeDtypeStruct((n_rows, d), F32),
        scratch_shapes=[pltpu.SemaphoreType.DMA(())],
        input_output_aliases={2: 0},
        compiler_params=_params("arbitrary"),
        name="moe_dispatch",
    )(pos, h32, jnp.zeros((n_rows, d), F32))


def _group_ffn_kernel(te_ref, nt_ref, x_ref, wg_ref, wu_ref, wd_ref, y_ref, acc_scr, x16_scr):
    del te_ref
    j = pl.program_id(1)
    last = pl.num_programs(1) - 1
    live = pl.program_id(0) < nt_ref[0]

    @pl.when(live & (j == 0))
    def _():
        x16_scr[...] = x_ref[...].astype(BF16)
        acc_scr[...] = jnp.zeros_like(acc_scr)

    @pl.when(live)
    def _():
        x = x16_scr[...]
        a = _silu(_dot(x, wg_ref[...])) * _dot(x, wu_ref[...])
        acc_scr[...] += _dot(a.astype(BF16), wd_ref[...])

    @pl.when(live & (j == last))
    def _():
        y_ref[...] = acc_scr[...]

    @pl.when(jnp.logical_not(live) & (j == last))
    def _():
        y_ref[...] = jnp.zeros_like(y_ref)


def _group_ffn(tile_expert, n_tiles, xg, wg, wu, wd, tm):
    n_rows, d = xg.shape
    f = wg.shape[2]
    tf = _ff_tile(f)
    nf = f // tf

    def col(i, j, te, nt):
        return jnp.where(i < nt[0], j, nf - 1)

    grid_spec = pltpu.PrefetchScalarGridSpec(
        num_scalar_prefetch=2,
        grid=(n_rows // tm, nf),
        in_specs=[pl.BlockSpec((tm, d), lambda i, j, te, nt: (i, 0)),
                  pl.BlockSpec((None, d, tf), lambda i, j, te, nt: (te[i], 0, col(i, j, te, nt))),
                  pl.BlockSpec((None, d, tf), lambda i, j, te, nt: (te[i], 0, col(i, j, te, nt))),
                  pl.BlockSpec((None, tf, d), lambda i, j, te, nt: (te[i], col(i, j, te, nt), 0))],
        out_specs=pl.BlockSpec((tm, d), lambda i, j, te, nt: (i, 0)),
        scratch_shapes=[pltpu.VMEM((tm, d), F32), pltpu.VMEM((tm, d), BF16)],
    )
    return pl.pallas_call(
        _group_ffn_kernel,
        grid_spec=grid_spec,
        out_shape=jax.ShapeDtypeStruct((n_rows, d), F32),
        compiler_params=_params("arbitrary", "arbitrary"),
        name="moe_ffn",
    )(tile_expert, n_tiles, xg, wg, wu, wd)


def _combine_kernel(pos_ref, r_ref, h32_ref, y_ref, lg_ref, lb_ref, o32_ref, o16_ref, g_scr, sem, *, alpha):
    tm = h32_ref.shape[0]

    def row_copy(r, s):
        return pltpu.make_async_copy(y_ref.at[pl.ds(pos_ref[s, r], 1)], g_scr.at[s, pl.ds(r, 1)], sem)

    _for_each_row(tm, lambda r: [row_copy(r, s).start() for s in range(TOP_K)])
    _for_each_row(tm, lambda r: [row_copy(r, s).wait() for s in range(TOP_K)])
    rec = r_ref[...]
    f = rec[:, R_W1:R_W1 + 1] * g_scr[0] + rec[:, R_W2:R_W2 + 1] * g_scr[1]
    y = _layer_norm(alpha * h32_ref[...] + f, lg_ref[...], lb_ref[...])
    o32_ref[...] = y
    o16_ref[...] = y.astype(BF16)


def _combine(pos, rec, h32, yg, lg, lb, alpha):
    n, d = h32.shape
    tm = _row_tile(n, 256)
    row = pl.BlockSpec((tm, d), lambda i: (i, 0))
    return pl.pallas_call(
        functools.partial(_combine_kernel, alpha=alpha),
        grid=(n // tm,),
        in_specs=[pl.BlockSpec((TOP_K, tm), lambda i: (0, i), memory_space=pltpu.SMEM),
                  pl.BlockSpec((tm, LANES), lambda i: (i, 0)), row,
                  pl.BlockSpec(memory_space=pl.ANY), _const_spec(lg.shape), _const_spec(lb.shape)],
        out_specs=[row, row],
        out_shape=[jax.ShapeDtypeStruct((n, d), F32), jax.ShapeDtypeStruct((n, d), BF16)],
        scratch_shapes=[pltpu.VMEM((TOP_K, tm, d), F32), pltpu.SemaphoreType.DMA(())],
        compiler_params=_params("arbitrary"),
        name="moe_combine",
    )(pos, rec, h32, yg, lg, lb)


def _moe(h32, w_router, wg, wu, wd, lg, lb, alpha):
    n, d = h32.shape
    n_experts = wg.shape[0]
    tm = 1024 if n >= 8192 else 256
    max_tiles = (TOP_K * n) // tm + n_experts
    rec, counts = _route(h32, w_router)

    cnt = counts[0, :n_experts].astype(jnp.int32)
    ends = jnp.cumsum((cnt + tm - 1) // tm * tm)
    starts = ends - (cnt + tm - 1) // tm * tm
    expert = rec[:, R_E1:R_E2 + 1].astype(jnp.int32)
    rank = rec[:, R_RANK1:R_RANK2 + 1].astype(jnp.int32)
    pos = (starts[expert] + rank).T
    n_tiles = (ends[-1] // tm).astype(jnp.int32)
    tile_start = jnp.arange(max_tiles, dtype=jnp.int32) * tm
    tile_expert = jnp.minimum(jnp.searchsorted(ends, tile_start, side='right'), n_experts - 1).astype(jnp.int32)
    tile_expert = jnp.where(jnp.arange(max_tiles) < n_tiles, tile_expert, tile_expert[jnp.maximum(n_tiles - 1, 0)])

    xg = _dispatch(pos, h32, max_tiles * tm)
    yg = _group_ffn(tile_expert, n_tiles.reshape(1), xg, wg, wu, wd, tm)
    return _combine(pos, rec, h32, yg, lg, lb, alpha)


def _layer_weights(w, l, d, mix_w):
    rank = w['w_a2'].shape[1]
    qk_w = w['w_a2'].shape[2]
    sizes = (qk_w, qk_w, mix_w, mix_w, rank, mix_w, mix_w, mix_w, FOX_HEADS, mix_w, mix_w, 3 * d)
    offs = [0]
    for s in sizes:
        offs.append(offs[-1] + s)
    w_in = w['w_in'][l]
    col = lambda a, b: w_in[:, offs[a]:offs[b]]
    pad_cols = lambda x: jnp.pad(x, ((0, 0), (0, LANES - x.shape[1])))
    return dict(
        gla_wz=col(0, 4).astype(BF16),
        gla_wa1=pad_cols(col(4, 5)).astype(BF16),
        gla_wa2=jnp.pad(w['w_a2'][l], ((0, LANES - rank), (0, 0))).astype(BF16),
        gla_ba=w['b_a'][l].reshape(1, -1),
        gla_gn=w['gla_norm_g'][l].reshape(1, -1),
        fox_w=col(5, 8).astype(BF16),
        fox_wf=pad_cols(col(8, 9)).astype(BF16),
        fox_bf=pad_cols(w['b_f'][l].reshape(1, -1)),
        gm_w=col(9, 11).astype(BF16),
        gm_g=w['gmlp_norm_g'][l].reshape(1, -1),
        gm_b=w['gmlp_norm_b'][l].reshape(1, -1),
        gate_w=col(11, 12).astype(BF16),
        branch_w=w['w_branch'][l].astype(BF16),
        out_w=w['w_out'][l].astype(BF16),
    )


def _trunk(x, w, gla_s, fox_k, fox_v, fox_lf, keep_gmlp_rows):
    bsz, seq, d = x.shape
    depth = w['w_in'].shape[0]
    mix_w = w['gla_norm_g'].shape[1]
    alpha = (2 * depth) ** 0.25
    n = bsz * seq
    dk = w['w_a2'].shape[2] // GLA_HEADS
    dv = mix_w // GLA_HEADS
    dh = mix_w // FOX_HEADS
    gm_chunk = GMLP_CHUNK if seq % GMLP_CHUNK == 0 else seq

    h32, h16 = _ln_in(x.reshape(n, d), w['ln_in_g'], w['ln_in_b'])
    if fox_k is not None:
        cache_kt = jnp.transpose(fox_k, (0, 1, 3, 4, 2))
        cache_vt = jnp.transpose(fox_v, (0, 1, 3, 4, 2))
        cache_lf_t = jnp.transpose(fox_lf, (0, 1, 3, 2))
    s_out, gm_out = [], []
    fox_out = [jnp.zeros((depth, bsz, seq, FOX_HEADS, dh), F32), jnp.zeros((depth, bsz, seq, FOX_HEADS, dh), F32),
               jnp.zeros((depth, bsz, seq, FOX_HEADS), F32)]
    for l in range(depth):
        lw = _layer_weights(w, l, d, mix_w)
        h16_seq = h16.reshape(bsz, seq, d)

        if gla_s is None:
            s0_t = jnp.zeros((bsz, GLA_HEADS, dv, dk), F32)
        else:
            s0_t = jnp.swapaxes(gla_s[l], -1, -2)
        o_gla, s_t = _gla(h16_seq, lw['gla_wz'], lw['gla_wa1'], lw['gla_wa2'], lw['gla_ba'], lw['gla_gn'], s0_t)
        s_out.append(jnp.swapaxes(s_t, -1, -2))

        gm = _gmlp(h16_seq, lw['gm_w'], lw['gm_g'], lw['gm_b'], w['gmlp_ws'][l][:, :gm_chunk, :gm_chunk],
                   w['gmlp_bs'][l][:, :gm_chunk].T, keep_gmlp_rows)
        o_gm = gm[0]
        if keep_gmlp_rows:
            gm_out.append(gm[1])

        if fox_k is None:
            d0 = jnp.zeros((bsz, 1, LANES), F32)
        else:
            dcum, d0 = _fox_cumlog(cache_lf_t, l)
        *fox_out, q_t, kn, vn_t = _fox_proj(h16_seq, lw['fox_w'], lw['fox_wf'], lw['fox_bf'], d0, l, fox_out)
        if fox_k is None:
            o_fox = _fox_attn(q_t, kn, vn_t, kn, vn_t, dh=dh)
        else:
            o_fox = _fox_decode(q_t, cache_kt, cache_vt, dcum, kn, vn_t, l, dh=dh)

        h32, h16 = _merge(h32, h16, o_gla.reshape(n, mix_w), o_gm.reshape(n, mix_w), o_fox.reshape(n, mix_w),
                          lw['gate_w'], lw['branch_w'], lw['out_w'],
                          w['ln_g'][l, 0].reshape(1, d), w['ln_b'][l, 0].reshape(1, d), alpha)
        lg, lb = w['ln_g'][l, 1].reshape(1, d), w['ln_b'][l, 1].reshape(1, d)
        j = l // 2
        if l % 2 == 0:
            h32, h16 = _ffn(h32, h16, w['ffn_w_gate'][j].astype(BF16), w['ffn_w_up'][j].astype(BF16),
                            w['ffn_w_down'][j].astype(BF16), lg, lb, alpha)
        else:
            h32, h16 = _moe(h32, w['moe_router'][j], w['moe_w_gate'][j].astype(BF16),
                            w['moe_w_up'][j].astype(BF16), w['moe_w_down'][j].astype(BF16), lg, lb, alpha)
    gm_stack = jnp.stack(gm_out) if keep_gmlp_rows else None
    return (h32.reshape(bsz, seq, d), jnp.stack(s_out), *fox_out, gm_stack)


def kernel(x_prompt, x_sample, state_gla, cache_fox_k, cache_fox_v, cache_fox_logf, ln_in_g, ln_in_b, w_in, w_a2, b_a, gla_norm_g, b_f, gmlp_norm_g, gmlp_norm_b, gmlp_ws, gmlp_bs, w_branch, w_out, ln_g, ln_b, ffn_w_gate, ffn_w_up, ffn_w_down, moe_router, moe_w_gate, moe_w_up, moe_w_down):
    w = dict(ln_in_g=ln_in_g, ln_in_b=ln_in_b, w_in=w_in, w_a2=w_a2, b_a=b_a, gla_norm_g=gla_norm_g,
             b_f=b_f, gmlp_norm_g=gmlp_norm_g, gmlp_norm_b=gmlp_norm_b, gmlp_ws=gmlp_ws, gmlp_bs=gmlp_bs,
             w_branch=w_branch, w_out=w_out, ln_g=ln_g, ln_b=ln_b, ffn_w_gate=ffn_w_gate,
             ffn_w_up=ffn_w_up, ffn_w_down=ffn_w_down, moe_router=moe_router, moe_w_gate=moe_w_gate,
             moe_w_up=moe_w_up, moe_w_down=moe_w_down)
    y_p, gla_p, fk_p, fv_p, flf_p, _ = _trunk(x_prompt, w, None, None, None, None, False)
    y_s, gla_s, fk_s, fv_s, flf_s, gmv_s = _trunk(x_sample, w, state_gla, cache_fox_k, cache_fox_v,
                                                  cache_fox_logf, True)
    return (y_p, y_s, gla_p, fk_p, fv_p, flf_p, gla_s, fk_s, fv_s, flf_s, gmv_s)
```

```python
import functools

import jax
import jax.numpy as jnp
from jax import lax
from jax.experimental import pallas as pl
from jax.experimental.pallas import tpu as pltpu

F32 = jnp.float32
BF16 = jnp.bfloat16

GLA_HEADS = 4
GLA_CHUNK = 64
GLA_TAU = 16.0
FOX_HEADS = 8
GMLP_GROUPS = 4
GMLP_CHUNK = 128
TOP_K = 2
ATTN_GROUPS = (16, 8, 4, 2, 1)
V_ROWS = 80
DMA_UNROLL = 8
LN_EPS = 1e-5
RMS_EPS = 1e-6

V7X_VMEM_BYTES = 64 * 1024 * 1024
VMEM_LIMIT_BYTES = V7X_VMEM_BYTES * 3 // 4
FF_TILE_CAP = 512
LANES = 128


def _params(*semantics, flags=None):
    return pltpu.CompilerParams(dimension_semantics=semantics, vmem_limit_bytes=VMEM_LIMIT_BYTES, flags=flags)


def _const_spec(shape):
    zeros = (0,) * len(shape)
    return pl.BlockSpec(shape, lambda *_: zeros)


def _dot(a, b):
    return jnp.dot(a, b, preferred_element_type=F32)


def _dot_nt(a, b):
    return lax.dot_general(a, b, (((1,), (1,)), ((), ())), preferred_element_type=F32)


def _dot_tn(a, b):
    return lax.dot_general(a, b, (((0,), (0,)), ((), ())), preferred_element_type=F32)


def _layer_norm(x, g, b):
    mu = jnp.mean(x, axis=-1, keepdims=True)
    xc = x - mu
    var = jnp.mean(xc * xc, axis=-1, keepdims=True)
    return xc * lax.rsqrt(var + LN_EPS) * g + b


def _log_sigmoid(x):
    return -(jnp.maximum(-x, 0.0) + jnp.log1p(jnp.exp(-jnp.abs(x))))


def _silu(x):
    return x * jax.nn.sigmoid(x)


def _split3(x):
    hi = x.astype(BF16)
    r1 = x - hi.astype(F32)
    mid = r1.astype(BF16)
    lo = (r1 - mid.astype(F32)).astype(BF16)
    return hi, mid, lo


def _tri_cumsum(tri_bf, x):
    hi, mid, lo = _split3(x)
    return _dot(tri_bf, hi) + _dot(tri_bf, mid) + _dot(tri_bf, lo)


def _lower_tri(n):
    row = lax.broadcasted_iota(jnp.int32, (n, n), 0)
    col = lax.broadcasted_iota(jnp.int32, (n, n), 1)
    return row >= col


def _row_tile(n, cap):
    t = min(n, cap)
    assert n % t == 0, (n, cap)
    return t


def _ff_tile(f):
    for t in range(FF_TILE_CAP, 0, -LANES):
        if f % t == 0:
            return t
    return f


def _ln_in_kernel(x_ref, g_ref, b_ref, o32_ref, o16_ref):
    y = _layer_norm(x_ref[...], g_ref[...], b_ref[...])
    o32_ref[...] = y
    o16_ref[...] = y.astype(BF16)


def _ln_in(x2, g, b):
    n, d = x2.shape
    tm = _row_tile(n, 512)
    row = pl.BlockSpec((tm, d), lambda i: (i, 0))
    return pl.pallas_call(
        _ln_in_kernel,
        grid=(n // tm,),
        in_specs=[row, _const_spec((1, d)), _const_spec((1, d))],
        out_specs=[row, row],
        out_shape=[jax.ShapeDtypeStruct((n, d), F32), jax.ShapeDtypeStruct((n, d), BF16)],
        compiler_params=_params("parallel"),
        name="ln_in",
    )(x2, g.reshape(1, d), b.reshape(1, d))


def _gla_kernel(h_ref, wz_ref, wa1_ref, wa2_ref, ba_ref, gn_ref, s0_ref, o_ref, sfin_ref, s_scr,
                *, chunk, n_chunk, dk, dv):
    t = pl.program_id(1)

    @pl.when(t == 0)
    def _():
        s_scr[...] = s0_ref[...]

    qk_w = GLA_HEADS * dk
    v_w = GLA_HEADS * dv
    x = h_ref[...]
    z = _dot(x, wz_ref[...])
    a1 = _dot(x, wa1_ref[...])
    log_a = _log_sigmoid(_dot(a1.astype(BF16), wa2_ref[...]) + ba_ref[...]) * (1.0 / GLA_TAU)

    tri = _lower_tri(chunk)
    tri_bf = jnp.where(tri, 1.0, 0.0).astype(BF16)
    for c in range(n_chunk):
        rows = slice(c * chunk, (c + 1) * chunk)
        b = _tri_cumsum(tri_bf, log_a[rows])
        b_last = b[chunk - 1:chunk, :]
        q = z[rows, 0:qk_w]
        k = z[rows, qk_w:2 * qk_w]
        v = z[rows, 2 * qk_w:2 * qk_w + v_w]
        r = z[rows, 2 * qk_w + v_w:2 * qk_w + 2 * v_w]
        qe = (q * (dk ** -0.5)) * jnp.exp(b)
        ke = k * jnp.exp(-b)
        kd = k * jnp.exp(b_last - b)
        decay = jnp.exp(b_last)
        for hd in range(GLA_HEADS):
            kl = slice(hd * dk, (hd + 1) * dk)
            vl = slice(hd * dv, (hd + 1) * dv)
            qe_h = qe[:, kl].astype(BF16)
            v_h = v[:, vl].astype(BF16)
            a = jnp.where(tri, _dot_nt(qe_h, ke[:, kl].astype(BF16)), 0.0)
            s_t = s_scr[hd]
            o = _dot(a.astype(BF16), v_h) + _dot_nt(qe_h, s_t.astype(BF16))
            s_scr[hd] = s_t * decay[:, kl] + _dot_tn(v_h, kd[:, kl].astype(BF16))
            o = o * lax.rsqrt(jnp.mean(o * o, axis=-1, keepdims=True) + RMS_EPS) * gn_ref[:, vl]
            o_ref[rows, vl] = (o * _silu(r[:, vl])).astype(o_ref.dtype)

    @pl.when(t == pl.num_programs(1) - 1)
    def _():
        sfin_ref[...] = s_scr[...]


def _gla(h16, wz, wa1, wa2, ba, gn, s0_t):
    bsz, seq, d = h16.shape
    _, heads, dv, dk = s0_t.shape
    chunk = GLA_CHUNK if seq % GLA_CHUNK == 0 else seq
    tc = _row_tile(seq, 512)
    kern = functools.partial(_gla_kernel, chunk=chunk, n_chunk=tc // chunk, dk=dk, dv=dv)
    state = pl.BlockSpec((None, heads, dv, dk), lambda b, t: (b, 0, 0, 0))
    return pl.pallas_call(
        kern,
        grid=(bsz, seq // tc),
        in_specs=[pl.BlockSpec((None, tc, d), lambda b, t: (b, t, 0)),
                  _const_spec(wz.shape), _const_spec(wa1.shape), _const_spec(wa2.shape),
                  _const_spec(ba.shape), _const_spec(gn.shape), state],
        out_specs=[pl.BlockSpec((None, tc, heads * dv), lambda b, t: (b, t, 0)), state],
        out_shape=[jax.ShapeDtypeStruct((bsz, seq, heads * dv), BF16),
                   jax.ShapeDtypeStruct(s0_t.shape, F32)],
        scratch_shapes=[pltpu.VMEM((heads, dv, dk), F32)],
        compiler_params=_params("parallel", "arbitrary"),
        name="gla",
    )(h16, wz, wa1, wa2, ba, gn, s0_t)


def _gmlp_kernel(h_ref, w_ref, g_ref, b_ref, ws_ref, bs_ref, o_ref, *v_ref, chunk, n_chunk, width):
    x = h_ref[...]
    z = _dot(x, w_ref[...])
    u = jax.nn.gelu(z[:, :width])
    v = _layer_norm(jax.nn.gelu(z[:, width:]), g_ref[...], b_ref[...])
    if v_ref:
        v_ref[0][...] = v
    gdim = width // GMLP_GROUPS
    tri = _lower_tri(chunk)
    for g in range(GMLP_GROUPS):
        w_g = jnp.where(tri, ws_ref[g], 0.0).astype(BF16)
        bias = bs_ref[:, g:g + 1]
        cols = slice(g * gdim, (g + 1) * gdim)
        for n in range(n_chunk):
            rows = slice(n * chunk, (n + 1) * chunk)
            mixed = _dot(w_g, v[rows, cols].astype(BF16)) + bias
            o_ref[rows, cols] = (u[rows, cols] * mixed).astype(o_ref.dtype)


def _gmlp(h16, w, g, b, ws, bs_t, keep_v):
    bsz, seq, d = h16.shape
    width = g.shape[-1]
    chunk = ws.shape[-1]
    tc = _row_tile(seq, 512)
    kern = functools.partial(_gmlp_kernel, chunk=chunk, n_chunk=tc // chunk, width=width)
    tile = pl.BlockSpec((None, tc, width), lambda i, t: (i, t, 0))
    out_specs = [tile]
    out_shape = [jax.ShapeDtypeStruct((bsz, seq, width), BF16)]
    if keep_v:
        out_specs.append(tile)
        out_shape.append(jax.ShapeDtypeStruct((bsz, seq, width), F32))
    return pl.pallas_call(
        kern,
        grid=(bsz, seq // tc),
        in_specs=[pl.BlockSpec((None, tc, d), lambda i, t: (i, t, 0)),
                  _const_spec(w.shape), _const_spec(g.shape), _const_spec(b.shape),
                  _const_spec(ws.shape), _const_spec(bs_t.shape)],
        out_specs=out_specs,
        out_shape=out_shape,
        compiler_params=_params("parallel", "parallel"),
        name="gmlp",
    )(h16, w, g, b, ws, bs_t)


def _transpose_bf16(x):
    r, c = x.shape
    if r % LANES == 0 and c % LANES == 0:
        return x.T
    row = lax.broadcasted_iota(jnp.int32, (r, r), 0)
    col = lax.broadcasted_iota(jnp.int32, (r, r), 1)
    return _dot_tn(x.astype(BF16), jnp.where(row == col, 1.0, 0.0).astype(BF16))


def _fox_pack(dcum, zq, zk, zv, q_ref, k_ref, v_ref, *, dh):
    tm = dcum.shape[0]
    hi = dcum.astype(BF16).astype(F32)
    r1 = dcum - hi
    mid = r1.astype(BF16).astype(F32)
    lo = (r1 - mid).astype(BF16).astype(F32)
    lane = lax.broadcasted_iota(jnp.int32, (tm, LANES), 1)
    sub = lax.broadcasted_iota(jnp.int32, (8, tm), 0)
    pad = jnp.zeros((LANES - dh - 8, tm), F32)
    ones_row = jnp.where(sub == 0, 1.0, 0.0)
    zv_t = _transpose_bf16(zv)
    if q_ref is not None:
        zq_t = _transpose_bf16(zq * (dh ** -0.5))
        hi_t, mid_t, lo_t = _transpose_bf16(hi), _transpose_bf16(mid), _transpose_bf16(lo)
    for h in range(FOX_HEADS):
        cols = slice((h // 2) * LANES, (h // 2 + 1) * LANES)
        odd = h % 2
        data = (lane >= dh) if odd else (lane < dh)
        slot = lane - (0 if odd else dh)
        d_hi = jnp.broadcast_to(hi[:, h:h + 1], (tm, LANES))
        d_mid = jnp.broadcast_to(mid[:, h:h + 1], (tm, LANES))
        d_lo = jnp.broadcast_to(lo[:, h:h + 1], (tm, LANES))
        ones_first = jnp.where(slot < 0, 0.0, jnp.where(slot < 3, 1.0, 0.0))
        k_extra = jnp.where(slot == 3, -d_hi, jnp.where(slot == 4, -d_mid, jnp.where(slot == 5, -d_lo, ones_first)))
        k_ref[h] = jnp.where(data, zk[:, cols], k_extra).astype(k_ref.dtype)
        rows = slice(h * dh, (h + 1) * dh)
        v_ref[h] = jnp.concatenate([zv_t[rows], ones_row, pad[:V_ROWS - dh - 8]], axis=0).astype(v_ref.dtype)
        if q_ref is not None:
            bias = jnp.where(sub == 0, jnp.broadcast_to(hi_t[h:h + 1], (8, tm)),
                             jnp.where(sub == 1, jnp.broadcast_to(mid_t[h:h + 1], (8, tm)),
                                       jnp.where(sub == 2, jnp.broadcast_to(lo_t[h:h + 1], (8, tm)),
                                                 jnp.where(sub < 6, 1.0, 0.0))))
            parts = [bias, pad, zq_t[rows]] if odd else [zq_t[rows], bias, pad]
            q_ref[h] = jnp.concatenate(parts, axis=0).astype(q_ref.dtype)


def _fox_proj_kernel(h_ref, w_ref, wf_ref, bf_ref, d0_ref, *refs, width):
    kout_ref, vout_ref, lf_ref, q_ref, k_ref, v_ref, d_scr = refs[-7:]
    t = pl.program_id(1)

    @pl.when(t == 0)
    def _():
        d_scr[...] = d0_ref[...]

    tm = h_ref.shape[0]
    x = h_ref[...]
    z = _dot(x, w_ref[...])
    zq, zk, zv = z[:, :width], z[:, width:2 * width], z[:, 2 * width:]
    kout_ref[...] = zk.reshape(kout_ref.shape)
    vout_ref[...] = zv.reshape(vout_ref.shape)
    lf = _log_sigmoid(_dot(x, wf_ref[...]) + bf_ref[...])
    lane = lax.broadcasted_iota(jnp.int32, lf.shape, 1)
    lf = jnp.where(lane < FOX_HEADS, lf, 0.0)
    lf_ref[...] = lf[:, :FOX_HEADS]
    tri_bf = jnp.where(_lower_tri(tm), 1.0, 0.0).astype(BF16)
    dcum = _tri_cumsum(tri_bf, lf) + d_scr[...]
    d_scr[...] = dcum[tm - 1:tm, :]
    _fox_pack(dcum, zq, zk, zv, q_ref, k_ref, v_ref, dh=width // FOX_HEADS)


def _fox_proj(h16, w, wf, bf, d0, layer, stacked):
    bsz, seq, d = h16.shape
    width = w.shape[1] // 3
    dh = width // FOX_HEADS
    tm = _row_tile(seq, 512)
    kern = functools.partial(_fox_proj_kernel, width=width)
    heads = pl.BlockSpec((None, None, tm, FOX_HEADS, dh), lambda b, t: (layer, b, t, 0, 0))
    rows = pl.BlockSpec((None, FOX_HEADS, tm, LANES), lambda b, t: (b, 0, t, 0))
    cols = pl.BlockSpec((None, FOX_HEADS, LANES, tm), lambda b, t: (b, 0, 0, t))
    rows_shape = jax.ShapeDtypeStruct((bsz, FOX_HEADS, seq, LANES), BF16)
    cols_shape = jax.ShapeDtypeStruct((bsz, FOX_HEADS, LANES, seq), BF16)
    v_cols = pl.BlockSpec((None, FOX_HEADS, V_ROWS, tm), lambda b, t: (b, 0, 0, t))
    v_cols_shape = jax.ShapeDtypeStruct((bsz, FOX_HEADS, V_ROWS, seq), BF16)
    in_specs = [pl.BlockSpec((None, tm, d), lambda b, t: (b, t, 0)),
                _const_spec(w.shape), _const_spec(wf.shape), _const_spec(bf.shape),
                pl.BlockSpec((None, 1, LANES), lambda b, t: (b, 0, 0))]
    args = [h16, w, wf, bf, d0]
    aliases = {len(args) + i: i for i in range(len(stacked))}
    in_specs += [pl.BlockSpec(memory_space=pl.ANY)] * len(stacked)
    args += list(stacked)
    return pl.pallas_call(
        kern,
        grid=(bsz, seq // tm),
        in_specs=in_specs,
        out_specs=[heads, heads, pl.BlockSpec((None, None, tm, FOX_HEADS), lambda b, t: (layer, b, t, 0)),
                   cols, rows, v_cols],
        out_shape=[jax.ShapeDtypeStruct(s.shape, s.dtype) for s in stacked] + [cols_shape, rows_shape, v_cols_shape],
        scratch_shapes=[pltpu.VMEM((1, LANES), F32)],
        input_output_aliases=aliases,
        compiler_params=_params("parallel", "arbitrary"),
        name="fox_proj",
    )(*args)


def _fox_cumlog_kernel(lf_ref, d_ref, dend_ref):
    x = lf_ref[...]
    heads, past = x.shape
    lane = lax.broadcasted_iota(jnp.int32, x.shape, 1)
    shift = 1
    while shift < past:
        x = x + jnp.where(lane >= shift, pltpu.roll(x, shift, 1), 0.0)
        shift *= 2
    d_ref[...] = x
    total = jnp.broadcast_to(x[:, past - 1:past], (heads, LANES))
    sub = lax.broadcasted_iota(jnp.int32, (heads, LANES), 0)
    col = lax.broadcasted_iota(jnp.int32, (heads, LANES), 1)
    dend_ref[...] = jnp.sum(jnp.where(sub == col, total, 0.0), axis=0, keepdims=True)


def _fox_cumlog(lf_t, layer):
    _, bsz, heads, past = lf_t.shape
    return pl.pallas_call(
        _fox_cumlog_kernel,
        grid=(bsz,),
        in_specs=[pl.BlockSpec((None, None, heads, past), lambda b: (layer, b, 0, 0))],
        out_specs=[pl.BlockSpec((None, heads, past), lambda b: (b, 0, 0)),
                   pl.BlockSpec((None, 1, LANES), lambda b: (b, 0, 0))],
        out_shape=[jax.ShapeDtypeStruct((bsz, heads, past), F32), jax.ShapeDtypeStruct((bsz, 1, LANES), F32)],
        compiler_params=_params("parallel"),
        name="fox_cumlog",
    )(lf_t)


def _fox_decode_kernel(q_ref, kt_ref, vt_ref, d_ref, kn_ref, vn_ref, o_ref, *, dh):
    j = pl.program_id(1)
    tq = q_ref.shape[2]
    past = kt_ref.shape[2]
    sub = lax.broadcasted_iota(jnp.int32, (8, past), 0)
    pad = jnp.zeros((LANES - dh - 8, past), F32)
    ones_blk = jnp.where(sub == 0, 1.0, 0.0)
    causal = lax.broadcasted_iota(jnp.int32, (tq, tq), 1) <= lax.broadcasted_iota(jnp.int32, (tq, tq), 0)
    lane = lax.broadcasted_iota(jnp.int32, (tq, LANES), 1)
    outs = []
    for hh in range(2):
        odd = hh == 1
        d = d_ref[pl.ds(2 * j + hh, 1), :]
        hi = d.astype(BF16).astype(F32)
        r1 = d - hi
        mid = r1.astype(BF16).astype(F32)
        lo = (r1 - mid).astype(BF16).astype(F32)
        bias = jnp.where(sub == 3, -hi, jnp.where(sub == 4, -mid, jnp.where(sub == 5, -lo,
                         jnp.where(sub < 3, 1.0, 0.0))))
        kt, vt = kt_ref[hh], vt_ref[hh]
        k_ext = jnp.concatenate([bias, pad, kt] if odd else [kt, bias, pad], axis=0).astype(BF16)
        v_ext = jnp.concatenate([ones_blk, pad, vt] if odd else [vt, ones_blk, pad], axis=0).astype(BF16)
        qt = q_ref[hh]
        s_c = _dot_tn(qt, k_ext)
        kn_t = _transpose_bf16(kn_ref[hh].astype(F32)).astype(BF16)
        s_n = jnp.where(causal, _dot_tn(qt, kn_t), -jnp.inf)
        m = jnp.maximum(jnp.max(s_c, axis=-1, keepdims=True), jnp.max(s_n, axis=-1, keepdims=True))
        vn = vn_ref[hh].astype(F32)
        fill = jnp.zeros((LANES - dh - 8, tq), F32)
        vn = jnp.concatenate([vn[dh:dh + 8], fill, vn[:dh]] if odd else [vn[:dh + 8], fill], axis=0)
        acc = (_dot_nt(jnp.exp(s_c - m).astype(BF16), v_ext)
               + _dot_nt(jnp.exp(s_n - m).astype(BF16), vn.astype(BF16)))
        outs.append(acc / (acc[:, 0:1] if odd else acc[:, dh:dh + 1]))
    o_ref[...] = jnp.where(lane < dh, outs[0], outs[1]).astype(o_ref.dtype)


def _fox_decode(q_t, kt, vt, dcum, kn, vn_t, layer, *, dh):
    bsz, heads, _, seq = q_t.shape
    past = kt.shape[-1]
    new_t = pl.BlockSpec((None, 2, LANES, seq), lambda b, j: (b, j, 0, 0))
    cache = pl.BlockSpec((None, None, 2, dh, past), lambda b, j: (layer, b, j, 0, 0))
    return pl.pallas_call(
        functools.partial(_fox_decode_kernel, dh=dh),
        grid=(bsz, heads // 2),
        in_specs=[new_t, cache, cache, pl.BlockSpec((None, heads, past), lambda b, j: (b, 0, 0)),
                  pl.BlockSpec((None, 2, seq, LANES), lambda b, j: (b, j, 0, 0)),
                  pl.BlockSpec((None, 2, V_ROWS, seq), lambda b, j: (b, j, 0, 0))],
        out_specs=pl.BlockSpec((None, seq, LANES), lambda b, j: (b, 0, j)),
        out_shape=jax.ShapeDtypeStruct((bsz, seq, heads // 2 * LANES), BF16),
        compiler_params=_params("parallel", "parallel"),
        name="fox_decode",
    )(q_t, kt, vt, dcum, kn, vn_t)


def _fox_attn_kernel(q_ref, kp_ref, vp_ref, kn_ref, vn_ref, o_ref, s_scr, sd_scr, acc_scr,
                     *, tk, dh, full_base, full_per_q):
    tq = q_ref.shape[2]
    n_full = full_base + pl.program_id(2) * full_per_q
    q = [q_ref[hh] for hh in range(2)]
    causal = lax.broadcasted_iota(jnp.int32, (tq, tq), 0) <= lax.broadcasted_iota(jnp.int32, (tq, tq), 1)

    def scores(first, width):
        def step(i, ms):
            rows = pl.ds(pl.multiple_of(first + i * width, tk), width)
            out = []
            for hh in range(2):
                s = _dot(kp_ref[hh, rows, :], q[hh])
                s_scr[hh, rows, :] = s
                out.append(jnp.maximum(ms[hh], jnp.max(s, axis=0, keepdims=True)))
            return tuple(out)
        return step

    def accumulate(first, width):
        def step(i, carry):
            rows = pl.ds(pl.multiple_of(first + i * width, tk), width)
            for hh in range(2):
                p = jnp.exp(s_scr[hh, rows, :] - ms[hh]).astype(BF16)
                acc_scr[hh] += _dot(vp_ref[hh, :, rows], p)
            return carry
        return step

    runs, first, left = [], 0, n_full
    for group in ATTN_GROUPS:
        count = left // group
        runs.append((first, group * tk, count))
        first = first + count * group * tk
        left = left - count * group

    ms = tuple(jnp.full((1, tq), -jnp.inf, F32) for _ in range(2))
    for first, width, count in runs:
        ms = lax.fori_loop(0, count, scores(first, width), ms)
    ms = list(ms)
    for hh in range(2):
        s = jnp.where(causal, _dot(kn_ref[hh], q[hh]), -jnp.inf)
        sd_scr[hh] = s
        ms[hh] = jnp.maximum(ms[hh], jnp.max(s, axis=0, keepdims=True))
        acc_scr[hh] = _dot(vn_ref[hh], jnp.exp(sd_scr[hh] - ms[hh]).astype(BF16))
    for first, width, count in runs:
        lax.fori_loop(0, count, accumulate(first, width), 0)
    halves = []
    for hh in range(2):
        acc = acc_scr[hh]
        halves.append(acc[:dh] / acc[dh:dh + 1])
    o_ref[...] = _transpose_bf16(jnp.concatenate(halves, axis=0)).astype(o_ref.dtype)


def _fox_attn(q_t, kp, vp_t, kn, vn_t, *, dh):
    bsz, heads, _, seq = q_t.shape
    self_attn = kp is kn
    past = kp.shape[2]
    off = 0 if self_attn else past
    tq = _row_tile(seq, 2 * LANES)
    nq = seq // tq
    tk = tq if self_attn else _row_tile(past, 512)
    assert off % tk == 0 and (nq == 1 or tq % tk == 0), (off, tq, tk)
    full_base, full_per_q = off // tk, tq // tk
    max_full = max(full_base + (nq - 1) * full_per_q, 1)
    kern = functools.partial(_fox_attn_kernel, tk=tk, dh=dh, full_base=full_base, full_per_q=full_per_q)
    return pl.pallas_call(
        kern,
        grid=(bsz, heads // 2, nq),
        in_specs=[pl.BlockSpec((None, 2, LANES, tq), lambda b, j, i: (b, j, 0, i)),
                  pl.BlockSpec((None, 2, past, LANES), lambda b, j, i: (b, j, 0, 0)),
                  pl.BlockSpec((None, 2, V_ROWS, past), lambda b, j, i: (b, j, 0, 0)),
                  pl.BlockSpec((None, 2, tq, LANES), lambda b, j, i: (b, j, i, 0)),
                  pl.BlockSpec((None, 2, V_ROWS, tq), lambda b, j, i: (b, j, 0, i))],
        out_specs=pl.BlockSpec((None, tq, LANES), lambda b, j, i: (b, i, j)),
        out_shape=jax.ShapeDtypeStruct((bsz, seq, heads // 2 * LANES), BF16),
        scratch_shapes=[pltpu.VMEM((2, max_full * tk, tq), F32), pltpu.VMEM((2, tq, tq), F32),
                        pltpu.VMEM((2, V_ROWS, tq), F32)],
        compiler_params=_params("parallel", "parallel", "arbitrary"),
        name="fox_attn",
    )(q_t, kp, vp_t, kn, vn_t)


def _merge_kernel(h32_ref, h16_ref, og_ref, om_ref, of_ref, wg_ref, wb_ref, wo_ref, lg_ref, lb_ref,
                  o32_ref, o16_ref, *, alpha):
    d = h32_ref.shape[1]
    x = h16_ref[...]
    m = None
    for r, o_ref in enumerate((og_ref, om_ref, of_ref)):
        gate = jax.nn.sigmoid(_dot(x, wg_ref[:, r * d:(r + 1) * d]))
        term = gate * _dot(o_ref[...], wb_ref[r])
        m = term if m is None else m + term
    y = alpha * h32_ref[...] + _dot(m.astype(BF16), wo_ref[...])
    y = _layer_norm(y, lg_ref[...], lb_ref[...])
    o32_ref[...] = y
    o16_ref[...] = y.astype(BF16)


def _merge(h32, h16, o_gla, o_gm, o_fox, wg, wb, wo, lg, lb, alpha):
    n, d = h32.shape
    w = o_gla.shape[1]
    tm = _row_tile(n, 256)
    row = pl.BlockSpec((tm, d), lambda i: (i, 0))
    mix = pl.BlockSpec((tm, w), lambda i: (i, 0))
    return pl.pallas_call(
        functools.partial(_merge_kernel, alpha=alpha),
        grid=(n // tm,),
        in_specs=[row, row, mix, mix, mix, _const_spec(wg.shape), _const_spec(wb.shape),
                  _const_spec(wo.shape), _const_spec(lg.shape), _const_spec(lb.shape)],
        out_specs=[row, row],
        out_shape=[jax.ShapeDtypeStruct((n, d), F32), jax.ShapeDtypeStruct((n, d), BF16)],
        compiler_params=_params("parallel"),
        name="merge",
    )(h32, h16, o_gla, o_gm, o_fox, wg, wb, wo, lg, lb)


def _ffn_kernel(h32_ref, h16_ref, wg_ref, wu_ref, wd_ref, lg_ref, lb_ref, o32_ref, o16_ref, acc_scr, *, alpha):
    j = pl.program_id(1)

    @pl.when(j == 0)
    def _():
        acc_scr[...] = jnp.zeros_like(acc_scr)

    x = h16_ref[...]
    a = _silu(_dot(x, wg_ref[...])) * _dot(x, wu_ref[...])
    acc_scr[...] += _dot(a.astype(BF16), wd_ref[...])

    @pl.when(j == pl.num_programs(1) - 1)
    def _():
        y = _layer_norm(alpha * h32_ref[...] + acc_scr[...], lg_ref[...], lb_ref[...])
        o32_ref[...] = y
        o16_ref[...] = y.astype(BF16)


def _ffn(h32, h16, wg, wu, wd, lg, lb, alpha):
    n, d = h32.shape
    f = wg.shape[1]
    tm = _row_tile(n, 1024)
    tf = _ff_tile(f)
    row = pl.BlockSpec((tm, d), lambda i, j: (i, 0))
    return pl.pallas_call(
        functools.partial(_ffn_kernel, alpha=alpha),
        grid=(n // tm, f // tf),
        in_specs=[row, row, pl.BlockSpec((d, tf), lambda i, j: (0, j)), pl.BlockSpec((d, tf), lambda i, j: (0, j)),
                  pl.BlockSpec((tf, d), lambda i, j: (j, 0)), _const_spec(lg.shape), _const_spec(lb.shape)],
        out_specs=[row, row],
        out_shape=[jax.ShapeDtypeStruct((n, d), F32), jax.ShapeDtypeStruct((n, d), BF16)],
        scratch_shapes=[pltpu.VMEM((tm, d), F32)],
        compiler_params=_params("parallel", "arbitrary"),
        name="ffn",
    )(h32, h16, wg, wu, wd, lg, lb)


R_W1, R_W2, R_E1, R_E2, R_RANK1, R_RANK2 = range(6)


def _route_kernel(h_ref, w_ref, r_ref, cnt_ref, cnt_scr, *, n_experts):
    @pl.when(pl.program_id(0) == 0)
    def _():
        cnt_scr[...] = jnp.zeros_like(cnt_scr)

    tm = h_ref.shape[0]
    x_hi, x_lo, _ = _split3(h_ref[...])
    w_hi, w_lo, _ = _split3(w_ref[...])
    logits = _dot(x_hi, w_hi) + _dot(x_hi, w_lo) + _dot(x_lo, w_hi)
    lane = lax.broadcasted_iota(jnp.int32, logits.shape, 1)
    logits = jnp.where(lane < n_experts, logits, -jnp.inf)
    v1 = jnp.max(logits, axis=-1, keepdims=True)
    i1 = jnp.min(jnp.where(logits == v1, lane, LANES), axis=-1, keepdims=True)
    rest = jnp.where(lane == i1, -jnp.inf, logits)
    v2 = jnp.max(rest, axis=-1, keepdims=True)
    i2 = jnp.min(jnp.where(rest == v2, lane, LANES), axis=-1, keepdims=True)
    e2 = jnp.exp(v2 - v1)
    w1 = 1.0 / (1.0 + e2)
    w2 = e2 / (1.0 + e2)
    hot1 = lane == i1
    hot2 = lane == i2
    sel = jnp.where(hot1, 1.0, jnp.where(hot2, 1.0, 0.0))
    tri_bf = jnp.where(_lower_tri(tm), 1.0, 0.0).astype(BF16)
    incl = _dot(tri_bf, sel.astype(BF16)) + cnt_scr[...]
    excl = incl - sel
    rank1 = jnp.sum(jnp.where(hot1, excl, 0.0), axis=-1, keepdims=True)
    rank2 = jnp.sum(jnp.where(hot2, excl, 0.0), axis=-1, keepdims=True)
    cnt_scr[...] = incl[tm - 1:tm, :]
    cnt_ref[...] = incl[tm - 1:tm, :]
    rec = jnp.zeros(logits.shape, F32)
    for slot, val in ((R_W1, w1), (R_W2, w2), (R_E1, i1.astype(F32)), (R_E2, i2.astype(F32)),
                      (R_RANK1, rank1), (R_RANK2, rank2)):
        rec = jnp.where(lane == slot, val, rec)
    r_ref[...] = rec


def _route(h32, w_router):
    n, d = h32.shape
    n_experts = w_router.shape[1]
    w_pad = jnp.pad(w_router, ((0, 0), (0, LANES - n_experts)))
    tm = _row_tile(n, 512)
    return pl.pallas_call(
        functools.partial(_route_kernel, n_experts=n_experts),
        grid=(n // tm,),
        in_specs=[pl.BlockSpec((tm, d), lambda i: (i, 0)), _const_spec(w_pad.shape)],
        out_specs=[pl.BlockSpec((tm, LANES), lambda i: (i, 0)), _const_spec((1, LANES))],
        out_shape=[jax.ShapeDtypeStruct((n, LANES), F32), jax.ShapeDtypeStruct((1, LANES), F32)],
        scratch_shapes=[pltpu.VMEM((1, LANES), F32)],
        compiler_params=_params("arbitrary"),
        name="route",
    )(h32, w_pad)


def _for_each_row(n_rows, fn):
    assert n_rows % DMA_UNROLL == 0, n_rows

    def trip(i, carry):
        for j in range(DMA_UNROLL):
            fn(i * DMA_UNROLL + j)
        return carry

    lax.fori_loop(0, n_rows // DMA_UNROLL, trip, 0)


def _dispatch_kernel(pos_ref, h_ref, xin_ref, xg_ref, sem):
    del xin_ref
    tm = pos_ref.shape[1]

    def row_copy(r, s):
        return pltpu.make_async_copy(h_ref.at[pl.ds(r, 1)], xg_ref.at[pl.ds(pos_ref[s, r], 1)], sem)

    _for_each_row(tm, lambda r: [row_copy(r, s).start() for s in range(TOP_K)])
    _for_each_row(tm, lambda r: [row_copy(r, s).wait() for s in range(TOP_K)])


def _dispatch(pos, h32, n_rows):
    n, d = h32.shape
    tm = _row_tile(n, 1024)
    return pl.pallas_call(
        _dispatch_kernel,
        grid=(n // tm,),
        in_specs=[pl.BlockSpec((TOP_K, tm), lambda i: (0, i), memory_space=pltpu.SMEM),
                  pl.BlockSpec((tm, d), lambda i: (i, 0)), pl.BlockSpec(memory_space=pl.ANY)],
        out_specs=pl.BlockSpec(memory_space=pl.ANY),
        out_shape=jax.ShapeDtypeStruct((n_rows, d), F32),
        scratch_shapes=[pltpu.SemaphoreType.DMA(())],
        input_output_aliases={2: 0},
        compiler_params=_params("arbitrary"),
        name="moe_dispatch",
    )(pos, h32, jnp.zeros((n_rows, d), F32))


def _group_ffn_kernel(te_ref, nt_ref, x_ref, wg_ref, wu_ref, wd_ref, y_ref, acc_scr, x16_scr):
    del te_ref
    j = pl.program_id(1)
    last = pl.num_programs(1) - 1
    live = pl.program_id(0) < nt_ref[0]

    @pl.when(live & (j == 0))
    def _():
        x16_scr[...] = x_ref[...].astype(BF16)
        acc_scr[...] = jnp.zeros_like(acc_scr)

    @pl.when(live)
    def _():
        x = x16_scr[...]
        a = _silu(_dot(x, wg_ref[...])) * _dot(x, wu_ref[...])
        acc_scr[...] += _dot(a.astype(BF16), wd_ref[...])

    @pl.when(live & (j == last))
    def _():
        y_ref[...] = acc_scr[...]

    @pl.when(jnp.logical_not(live) & (j == last))
    def _():
        y_ref[...] = jnp.zeros_like(y_ref)


def _group_ffn(tile_expert, n_tiles, xg, wg, wu, wd, tm):
    n_rows, d = xg.shape
    f = wg.shape[2]
    tf = _ff_tile(f)
    nf = f // tf

    def col(i, j, te, nt):
        return jnp.where(i < nt[0], j, nf - 1)

    grid_spec = pltpu.PrefetchScalarGridSpec(
        num_scalar_prefetch=2,
        grid=(n_rows // tm, nf),
        in_specs=[pl.BlockSpec((tm, d), lambda i, j, te, nt: (i, 0)),
                  pl.BlockSpec((None, d, tf), lambda i, j, te, nt: (te[i], 0, col(i, j, te, nt))),
                  pl.BlockSpec((None, d, tf), lambda i, j, te, nt: (te[i], 0, col(i, j, te, nt))),
                  pl.BlockSpec((None, tf, d), lambda i, j, te, nt: (te[i], col(i, j, te, nt), 0))],
        out_specs=pl.BlockSpec((tm, d), lambda i, j, te, nt: (i, 0)),
        scratch_shapes=[pltpu.VMEM((tm, d), F32), pltpu.VMEM((tm, d), BF16)],
    )
    return pl.pallas_call(
        _group_ffn_kernel,
        grid_spec=grid_spec,
        out_shape=jax.ShapeDtypeStruct((n_rows, d), F32),
        compiler_params=_params("arbitrary", "arbitrary"),
        name="moe_ffn",
    )(tile_expert, n_tiles, xg, wg, wu, wd)


def _combine_kernel(pos_ref, r_ref, h32_ref, y_ref, lg_ref, lb_ref, o32_ref, o16_ref, g_scr, sem, *, alpha):
    tm = h32_ref.shape[0]

    def row_copy(r, s):
        return pltpu.make_async_copy(y_ref.at[pl.ds(pos_ref[s, r], 1)], g_scr.at[s, pl.ds(r, 1)], sem)

    _for_each_row(tm, lambda r: [row_copy(r, s).start() for s in range(TOP_K)])
    _for_each_row(tm, lambda r: [row_copy(r, s).wait() for s in range(TOP_K)])
    rec = r_ref[...]
    f = rec[:, R_W1:R_W1 + 1] * g_scr[0] + rec[:, R_W2:R_W2 + 1] * g_scr[1]
    y = _layer_norm(alpha * h32_ref[...] + f, lg_ref[...], lb_ref[...])
    o32_ref[...] = y
    o16_ref[...] = y.astype(BF16)


def _combine(pos, rec, h32, yg, lg, lb, alpha):
    n, d = h32.shape
    tm = _row_tile(n, 256)
    row = pl.BlockSpec((tm, d), lambda i: (i, 0))
    return pl.pallas_call(
        functools.partial(_combine_kernel, alpha=alpha),
        grid=(n // tm,),
        in_specs=[pl.BlockSpec((TOP_K, tm), lambda i: (0, i), memory_space=pltpu.SMEM),
                  pl.BlockSpec((tm, LANES), lambda i: (i, 0)), row,
                  pl.BlockSpec(memory_space=pl.ANY), _const_spec(lg.shape), _const_spec(lb.shape)],
        out_specs=[row, row],
        out_shape=[jax.ShapeDtypeStruct((n, d), F32), jax.ShapeDtypeStruct((n, d), BF16)],
        scratch_shapes=[pltpu.VMEM((TOP_K, tm, d), F32), pltpu.SemaphoreType.DMA(())],
        compiler_params=_params("arbitrary"),
        name="moe_combine",
    )(pos, rec, h32, yg, lg, lb)


def _moe(h32, w_router, wg, wu, wd, lg, lb, alpha):
    n, d = h32.shape
    n_experts = wg.shape[0]
    tm = 1024 if n >= 8192 else 256
    max_tiles = (TOP_K * n) // tm + n_experts
    rec, counts = _route(h32, w_router)

    cnt = counts[0, :n_experts].astype(jnp.int32)
    ends = jnp.cumsum((cnt + tm - 1) // tm * tm)
    starts = ends - (cnt + tm - 1) // tm * tm
    expert = rec[:, R_E1:R_E2 + 1].astype(jnp.int32)
    rank = rec[:, R_RANK1:R_RANK2 + 1].astype(jnp.int32)
    pos = (starts[expert] + rank).T
    n_tiles = (ends[-1] // tm).astype(jnp.int32)
    tile_start = jnp.arange(max_tiles, dtype=jnp.int32) * tm
    tile_expert = jnp.minimum(jnp.searchsorted(ends, tile_start, side='right'), n_experts - 1).astype(jnp.int32)
    tile_expert = jnp.where(jnp.arange(max_tiles) < n_tiles, tile_expert, tile_expert[jnp.maximum(n_tiles - 1, 0)])

    xg = _dispatch(pos, h32, max_tiles * tm)
    yg = _group_ffn(tile_expert, n_tiles.reshape(1), xg, wg, wu, wd, tm)
    return _combine(pos, rec, h32, yg, lg, lb, alpha)


def _layer_weights(w, l, d, mix_w):
    rank = w['w_a2'].shape[1]
    qk_w = w['w_a2'].shape[2]
    sizes = (qk_w, qk_w, mix_w, mix_w, rank, mix_w, mix_w, mix_w, FOX_HEADS, mix_w, mix_w, 3 * d)
    offs = [0]
    for s in sizes:
        offs.append(offs[-1] + s)
    w_in = w['w_in'][l]
    col = lambda a, b: w_in[:, offs[a]:offs[b]]
    pad_cols = lambda x: jnp.pad(x, ((0, 0), (0, LANES - x.shape[1])))
    return dict(
        gla_wz=col(0, 4).astype(BF16),
        gla_wa1=pad_cols(col(4, 5)).astype(BF16),
        gla_wa2=jnp.pad(w['w_a2'][l], ((0, LANES - rank), (0, 0))).astype(BF16),
        gla_ba=w['b_a'][l].reshape(1, -1),
        gla_gn=w['gla_norm_g'][l].reshape(1, -1),
        fox_w=col(5, 8).astype(BF16),
        fox_wf=pad_cols(col(8, 9)).astype(BF16),
        fox_bf=pad_cols(w['b_f'][l].reshape(1, -1)),
        gm_w=col(9, 11).astype(BF16),
        gm_g=w['gmlp_norm_g'][l].reshape(1, -1),
        gm_b=w['gmlp_norm_b'][l].reshape(1, -1),
        gate_w=col(11, 12).astype(BF16),
        branch_w=w['w_branch'][l].astype(BF16),
        out_w=w['w_out'][l].astype(BF16),
    )


def _trunk(x, w, gla_s, fox_k, fox_v, fox_lf, keep_gmlp_rows):
    bsz, seq, d = x.shape
    depth = w['w_in'].shape[0]
    mix_w = w['gla_norm_g'].shape[1]
    alpha = (2 * depth) ** 0.25
    n = bsz * seq
    dk = w['w_a2'].shape[2] // GLA_HEADS
    dv = mix_w // GLA_HEADS
    dh = mix_w // FOX_HEADS
    gm_chunk = GMLP_CHUNK if seq % GMLP_CHUNK == 0 else seq

    h32, h16 = _ln_in(x.reshape(n, d), w['ln_in_g'], w['ln_in_b'])
    if fox_k is not None:
        cache_kt = jnp.transpose(fox_k, (0, 1, 3, 4, 2))
        cache_vt = jnp.transpose(fox_v, (0, 1, 3, 4, 2))
        cache_lf_t = jnp.transpose(fox_lf, (0, 1, 3, 2))
    s_out, gm_out = [], []
    fox_out = [jnp.zeros((depth, bsz, seq, FOX_HEADS, dh), F32), jnp.zeros((depth, bsz, seq, FOX_HEADS, dh), F32),
               jnp.zeros((depth, bsz, seq, FOX_HEADS), F32)]
    for l in range(depth):
        lw = _layer_weights(w, l, d, mix_w)
        h16_seq = h16.reshape(bsz, seq, d)

        if gla_s is None:
            s0_t = jnp.zeros((bsz, GLA_HEADS, dv, dk), F32)
        else:
            s0_t = jnp.swapaxes(gla_s[l], -1, -2)
        o_gla, s_t = _gla(h16_seq, lw['gla_wz'], lw['gla_wa1'], lw['gla_wa2'], lw['gla_ba'], lw['gla_gn'], s0_t)
        s_out.append(jnp.swapaxes(s_t, -1, -2))

        gm = _gmlp(h16_seq, lw['gm_w'], lw['gm_g'], lw['gm_b'], w['gmlp_ws'][l][:, :gm_chunk, :gm_chunk],
                   w['gmlp_bs'][l][:, :gm_chunk].T, keep_gmlp_rows)
        o_gm = gm[0]
        if keep_gmlp_rows:
            gm_out.append(gm[1])

        if fox_k is None:
            d0 = jnp.zeros((bsz, 1, LANES), F32)
        else:
            dcum, d0 = _fox_cumlog(cache_lf_t, l)
        *fox_out, q_t, kn, vn_t = _fox_proj(h16_seq, lw['fox_w'], lw['fox_wf'], lw['fox_bf'], d0, l, fox_out)
        if fox_k is None:
            o_fox = _fox_attn(q_t, kn, vn_t, kn, vn_t, dh=dh)
        else:
            o_fox = _fox_decode(q_t, cache_kt, cache_vt, dcum, kn, vn_t, l, dh=dh)

        h32, h16 = _merge(h32, h16, o_gla.reshape(n, mix_w), o_gm.reshape(n, mix_w), o_fox.reshape(n, mix_w),
                          lw['gate_w'], lw['branch_w'], lw['out_w'],
                          w['ln_g'][l, 0].reshape(1, d), w['ln_b'][l, 0].reshape(1, d), alpha)
        lg, lb = w['ln_g'][l, 1].reshape(1, d), w['ln_b'][l, 1].reshape(1, d)
        j = l // 2
        if l % 2 == 0:
            h32, h16 = _ffn(h32, h16, w['ffn_w_gate'][j].astype(BF16), w['ffn_w_up'][j].astype(BF16),
                            w['ffn_w_down'][j].astype(BF16), lg, lb, alpha)
        else:
            h32, h16 = _moe(h32, w['moe_router'][j], w['moe_w_gate'][j].astype(BF16),
                            w['moe_w_up'][j].astype(BF16), w['moe_w_down'][j].astype(BF16), lg, lb, alpha)
    gm_stack = jnp.stack(gm_out) if keep_gmlp_rows else None
    return (h32.reshape(bsz, seq, d), jnp.stack(s_out), *fox_out, gm_stack)


def kernel(x_prompt, x_sample, state_gla, cache_fox_k, cache_fox_v, cache_fox_logf, ln_in_g, ln_in_b, w_in, w_a2, b_a, gla_norm_g, b_f, gmlp_norm_g, gmlp_norm_b, gmlp_ws, gmlp_bs, w_branch, w_out, ln_g, ln_b, ffn_w_gate, ffn_w_up, ffn_w_down, moe_router, moe_w_gate, moe_w_up, moe_w_down):
    w = dict(ln_in_g=ln_in_g, ln_in_b=ln_in_b, w_in=w_in, w_a2=w_a2, b_a=b_a, gla_norm_g=gla_norm_g,
             b_f=b_f, gmlp_norm_g=gmlp_norm_g, gmlp_norm_b=gmlp_norm_b, gmlp_ws=gmlp_ws, gmlp_bs=gmlp_bs,
             w_branch=w_branch, w_out=w_out, ln_g=ln_g, ln_b=ln_b, ffn_w_gate=ffn_w_gate,
             ffn_w_up=ffn_w_up, ffn_w_down=ffn_w_down, moe_router=moe_router, moe_w_gate=moe_w_gate,
             moe_w_up=moe_w_up, moe_w_down=moe_w_down)
    y_p, gla_p, fk_p, fv_p, flf_p, _ = _trunk(x_prompt, w, None, None, None, None, False)
    y_s, gla_s, fk_s, fv_s, flf_s, gmv_s = _trunk(x_sample, w, state_gla, cache_fox_k, cache_fox_v,
                                                  cache_fox_logf, True)
    return (y_p, y_s, gla_p, fk_p, fv_p, flf_p, gla_s, fk_s, fv_s, flf_s, gmv_s)
```

```python
import functools

import jax
import jax.numpy as jnp
from jax import lax
from jax.experimental import pallas as pl
from jax.experimental.pallas import tpu as pltpu

F32 = jnp.float32
BF16 = jnp.bfloat16

GLA_HEADS = 4
GLA_CHUNK = 64
GLA_TAU = 16.0
FOX_HEADS = 8
GMLP_GROUPS = 4
GMLP_CHUNK = 128
TOP_K = 2
ATTN_GROUPS = (16, 8, 4, 2, 1)
V_ROWS = 80
DMA_UNROLL = 8
LN_EPS = 1e-5
RMS_EPS = 1e-6

V7X_VMEM_BYTES = 64 * 1024 * 1024
VMEM_LIMIT_BYTES = V7X_VMEM_BYTES * 3 // 4
FF_TILE_CAP = 512
LANES = 128


def _params(*semantics, flags=None):
    return pltpu.CompilerParams(dimension_semantics=semantics, vmem_limit_bytes=VMEM_LIMIT_BYTES, flags=flags)


def _const_spec(shape):
    zeros = (0,) * len(shape)
    return pl.BlockSpec(shape, lambda *_: zeros)


def _dot(a, b):
    return jnp.dot(a, b, preferred_element_type=F32)


def _dot_nt(a, b):
    return lax.dot_general(a, b, (((1,), (1,)), ((), ())), preferred_element_type=F32)


def _dot_tn(a, b):
    return lax.dot_general(a, b, (((0,), (0,)), ((), ())), preferred_element_type=F32)


def _layer_norm(x, g, b):
    mu = jnp.mean(x, axis=-1, keepdims=True)
    xc = x - mu
    var = jnp.mean(xc * xc, axis=-1, keepdims=True)
    return xc * lax.rsqrt(var + LN_EPS) * g + b


def _log_sigmoid(x):
    return -(jnp.maximum(-x, 0.0) + jnp.log1p(jnp.exp(-jnp.abs(x))))


def _silu(x):
    return x * jax.nn.sigmoid(x)


def _split3(x):
    hi = x.astype(BF16)
    r1 = x - hi.astype(F32)
    mid = r1.astype(BF16)
    lo = (r1 - mid.astype(F32)).astype(BF16)
    return hi, mid, lo


def _tri_cumsum(tri_bf, x):
    hi, mid, lo = _split3(x)
    return _dot(tri_bf, hi) + _dot(tri_bf, mid) + _dot(tri_bf, lo)


def _lower_tri(n):
    row = lax.broadcasted_iota(jnp.int32, (n, n), 0)
    col = lax.broadcasted_iota(jnp.int32, (n, n), 1)
    return row >= col


def _row_tile(n, cap):
    t = min(n, cap)
    assert n % t == 0, (n, cap)
    return t


def _ff_tile(f):
    for t in range(FF_TILE_CAP, 0, -LANES):
        if f % t == 0:
            return t
    return f


def _ln_in_kernel(x_ref, g_ref, b_ref, o32_ref, o16_ref):
    y = _layer_norm(x_ref[...], g_ref[...], b_ref[...])
    o32_ref[...] = y
    o16_ref[...] = y.astype(BF16)


def _ln_in(x2, g, b):
    n, d = x2.shape
    tm = _row_tile(n, 512)
    row = pl.BlockSpec((tm, d), lambda i: (i, 0))
    return pl.pallas_call(
        _ln_in_kernel,
        grid=(n // tm,),
        in_specs=[row, _const_spec((1, d)), _const_spec((1, d))],
        out_specs=[row, row],
        out_shape=[jax.ShapeDtypeStruct((n, d), F32), jax.ShapeDtypeStruct((n, d), BF16)],
        compiler_params=_params("parallel"),
        name="ln_in",
    )(x2, g.reshape(1, d), b.reshape(1, d))


def _gla_kernel(h_ref, wz_ref, wa1_ref, wa2_ref, ba_ref, gn_ref, s0_ref, o_ref, sfin_ref, s_scr,
                *, chunk, n_chunk, dk, dv):
    t = pl.program_id(1)

    @pl.when(t == 0)
    def _():
        s_scr[...] = s0_ref[...]

    qk_w = GLA_HEADS * dk
    v_w = GLA_HEADS * dv
    x = h_ref[...]
    z = _dot(x, wz_ref[...])
    a1 = _dot(x, wa1_ref[...])
    log_a = _log_sigmoid(_dot(a1.astype(BF16), wa2_ref[...]) + ba_ref[...]) * (1.0 / GLA_TAU)

    tri = _lower_tri(chunk)
    tri_bf = jnp.where(tri, 1.0, 0.0).astype(BF16)
    for c in range(n_chunk):
        rows = slice(c * chunk, (c + 1) * chunk)
        b = _tri_cumsum(tri_bf, log_a[rows])
        b_last = b[chunk - 1:chunk, :]
        q = z[rows, 0:qk_w]
        k = z[rows, qk_w:2 * qk_w]
        v = z[rows, 2 * qk_w:2 * qk_w + v_w]
        r = z[rows, 2 * qk_w + v_w:2 * qk_w + 2 * v_w]
        qe = (q * (dk ** -0.5)) * jnp.exp(b)
        ke = k * jnp.exp(-b)
        kd = k * jnp.exp(b_last - b)
        decay = jnp.exp(b_last)
        for hd in range(GLA_HEADS):
            kl = slice(hd * dk, (hd + 1) * dk)
            vl = slice(hd * dv, (hd + 1) * dv)
            qe_h = qe[:, kl].astype(BF16)
            v_h = v[:, vl].astype(BF16)
            a = jnp.where(tri, _dot_nt(qe_h, ke[:, kl].astype(BF16)), 0.0)
            s_t = s_scr[hd]
            o = _dot(a.astype(BF16), v_h) + _dot_nt(qe_h, s_t.astype(BF16))
            s_scr[hd] = s_t * decay[:, kl] + _dot_tn(v_h, kd[:, kl].astype(BF16))
            o = o * lax.rsqrt(jnp.mean(o * o, axis=-1, keepdims=True) + RMS_EPS) * gn_ref[:, vl]
            o_ref[rows, vl] = (o * _silu(r[:, vl])).astype(o_ref.dtype)

    @pl.when(t == pl.num_programs(1) - 1)
    def _():
        sfin_ref[...] = s_scr[...]


def _gla(h16, wz, wa1, wa2, ba, gn, s0_t):
    bsz, seq, d = h16.shape
    _, heads, dv, dk = s0_t.shape
    chunk = GLA_CHUNK if seq % GLA_CHUNK == 0 else seq
    tc = _row_tile(seq, 512)
    kern = functools.partial(_gla_kernel, chunk=chunk, n_chunk=tc // chunk, dk=dk, dv=dv)
    state = pl.BlockSpec((None, heads, dv, dk), lambda b, t: (b, 0, 0, 0))
    return pl.pallas_call(
        kern,
        grid=(bsz, seq // tc),
        in_specs=[pl.BlockSpec((None, tc, d), lambda b, t: (b, t, 0)),
                  _const_spec(wz.shape), _const_spec(wa1.shape), _const_spec(wa2.shape),
                  _const_spec(ba.shape), _const_spec(gn.shape), state],
        out_specs=[pl.BlockSpec((None, tc, heads * dv), lambda b, t: (b, t, 0)), state],
        out_shape=[jax.ShapeDtypeStruct((bsz, seq, heads * dv), BF16),
                   jax.ShapeDtypeStruct(s0_t.shape, F32)],
        scratch_shapes=[pltpu.VMEM((heads, dv, dk), F32)],
        compiler_params=_params("parallel", "arbitrary"),
        name="gla",
    )(h16, wz, wa1, wa2, ba, gn, s0_t)


def _gmlp_kernel(h_ref, w_ref, g_ref, b_ref, ws_ref, bs_ref, o_ref, *v_ref, chunk, n_chunk, width):
    x = h_ref[...]
    z = _dot(x, w_ref[...])
    u = jax.nn.gelu(z[:, :width])
    v = _layer_norm(jax.nn.gelu(z[:, width:]), g_ref[...], b_ref[...])
    if v_ref:
        v_ref[0][...] = v
    gdim = width // GMLP_GROUPS
    tri = _lower_tri(chunk)
    for g in range(GMLP_GROUPS):
        w_g = jnp.where(tri, ws_ref[g], 0.0).astype(BF16)
        bias = bs_ref[:, g:g + 1]
        cols = slice(g * gdim, (g + 1) * gdim)
        for n in range(n_chunk):
            rows = slice(n * chunk, (n + 1) * chunk)
            mixed = _dot(w_g, v[rows, cols].astype(BF16)) + bias
            o_ref[rows, cols] = (u[rows, cols] * mixed).astype(o_ref.dtype)


def _gmlp(h16, w, g, b, ws, bs_t, keep_v):
    bsz, seq, d = h16.shape
    width = g.shape[-1]
    chunk = ws.shape[-1]
    tc = _row_tile(seq, 512)
    kern = functools.partial(_gmlp_kernel, chunk=chunk, n_chunk=tc // chunk, width=width)
    tile = pl.BlockSpec((None, tc, width), lambda i, t: (i, t, 0))
    out_specs = [tile]
    out_shape = [jax.ShapeDtypeStruct((bsz, seq, width), BF16)]
    if keep_v:
        out_specs.append(tile)
        out_shape.append(jax.ShapeDtypeStruct((bsz, seq, width), F32))
    return pl.pallas_call(
        kern,
        grid=(bsz, seq // tc),
        in_specs=[pl.BlockSpec((None, tc, d), lambda i, t: (i, t, 0)),
                  _const_spec(w.shape), _const_spec(g.shape), _const_spec(b.shape),
                  _const_spec(ws.shape), _const_spec(bs_t.shape)],
        out_specs=out_specs,
        out_shape=out_shape,
        compiler_params=_params("parallel", "parallel"),
        name="gmlp",
    )(h16, w, g, b, ws, bs_t)


def _transpose_bf16(x):
    r, c = x.shape
    if r % LANES == 0 and c % LANES == 0:
        return x.T
    row = lax.broadcasted_iota(jnp.int32, (r, r), 0)
    col = lax.broadcasted_iota(jnp.int32, (r, r), 1)
    return _dot_tn(x.astype(BF16), jnp.where(row == col, 1.0, 0.0).astype(BF16))


def _fox_pack(dcum, zq, zk, zv, q_ref, k_ref, v_ref, *, dh):
    tm = dcum.shape[0]
    hi = dcum.astype(BF16).astype(F32)
    r1 = dcum - hi
    mid = r1.astype(BF16).astype(F32)
    lo = (r1 - mid).astype(BF16).astype(F32)
    lane = lax.broadcasted_iota(jnp.int32, (tm, LANES), 1)
    sub = lax.broadcasted_iota(jnp.int32, (8, tm), 0)
    pad = jnp.zeros((LANES - dh - 8, tm), F32)
    ones_row = jnp.where(sub == 0, 1.0, 0.0)
    zv_t = _transpose_bf16(zv)
    if q_ref is not None:
        zq_t = _transpose_bf16(zq * (dh ** -0.5))
        hi_t, mid_t, lo_t = _transpose_bf16(hi), _transpose_bf16(mid), _transpose_bf16(lo)
    for h in range(FOX_HEADS):
        cols = slice((h // 2) * LANES, (h // 2 + 1) * LANES)
        odd = h % 2
        data = (lane >= dh) if odd else (lane < dh)
        slot = lane - (0 if odd else dh)
        d_hi = jnp.broadcast_to(hi[:, h:h + 1], (tm, LANES))
        d_mid = jnp.broadcast_to(mid[:, h:h + 1], (tm, LANES))
        d_lo = jnp.broadcast_to(lo[:, h:h + 1], (tm, LANES))
        ones_first = jnp.where(slot < 0, 0.0, jnp.where(slot < 3, 1.0, 0.0))
        k_extra = jnp.where(slot == 3, -d_hi, jnp.where(slot == 4, -d_mid, jnp.where(slot == 5, -d_lo, ones_first)))
        k_ref[h] = jnp.where(data, zk[:, cols], k_extra).astype(k_ref.dtype)
        rows = slice(h * dh, (h + 1) * dh)
        v_ref[h] = jnp.concatenate([zv_t[rows], ones_row, pad[:V_ROWS - dh - 8]], axis=0).astype(v_ref.dtype)
        if q_ref is not None:
            bias = jnp.where(sub == 0, jnp.broadcast_to(hi_t[h:h + 1], (8, tm)),
                             jnp.where(sub == 1, jnp.broadcast_to(mid_t[h:h + 1], (8, tm)),
                                       jnp.where(sub == 2, jnp.broadcast_to(lo_t[h:h + 1], (8, tm)),
                                                 jnp.where(sub < 6, 1.0, 0.0))))
            parts = [bias, pad, zq_t[rows]] if odd else [zq_t[rows], bias, pad]
            q_ref[h] = jnp.concatenate(parts, axis=0).astype(q_ref.dtype)


def _fox_proj_kernel(h_ref, w_ref, wf_ref, bf_ref, d0_ref, *refs, width):
    kout_ref, vout_ref, lf_ref, q_ref, k_ref, v_ref, d_scr = refs[-7:]
    t = pl.program_id(1)

    @pl.when(t == 0)
    def _():
        d_scr[...] = d0_ref[...]

    tm = h_ref.shape[0]
    x = h_ref[...]
    z = _dot(x, w_ref[...])
    zq, zk, zv = z[:, :width], z[:, width:2 * width], z[:, 2 * width:]
    kout_ref[...] = zk.reshape(kout_ref.shape)
    vout_ref[...] = zv.reshape(vout_ref.shape)
    lf = _log_sigmoid(_dot(x, wf_ref[...]) + bf_ref[...])
    lane = lax.broadcasted_iota(jnp.int32, lf.shape, 1)
    lf = jnp.where(lane < FOX_HEADS, lf, 0.0)
    lf_ref[...] = lf[:, :FOX_HEADS]
    tri_bf = jnp.where(_lower_tri(tm), 1.0, 0.0).astype(BF16)
    dcum = _tri_cumsum(tri_bf, lf) + d_scr[...]
    d_scr[...] = dcum[tm - 1:tm, :]
    _fox_pack(dcum, zq, zk, zv, q_ref, k_ref, v_ref, dh=width // FOX_HEADS)


def _fox_proj(h16, w, wf, bf, d0, layer, stacked):
    bsz, seq, d = h16.shape
    width = w.shape[1] // 3
    dh = width // FOX_HEADS
    tm = _row_tile(seq, 512)
    kern = functools.partial(_fox_proj_kernel, width=width)
    heads = pl.BlockSpec((None, None, tm, FOX_HEADS, dh), lambda b, t: (layer, b, t, 0, 0))
    rows = pl.BlockSpec((None, FOX_HEADS, tm, LANES), lambda b, t: (b, 0, t, 0))
    cols = pl.BlockSpec((None, FOX_HEADS, LANES, tm), lambda b, t: (b, 0, 0, t))
    rows_shape = jax.ShapeDtypeStruct((bsz, FOX_HEADS, seq, LANES), BF16)
    cols_shape = jax.ShapeDtypeStruct((bsz, FOX_HEADS, LANES, seq), BF16)
    v_cols = pl.BlockSpec((None, FOX_HEADS, V_ROWS, tm), lambda b, t: (b, 0, 0, t))
    v_cols_shape = jax.ShapeDtypeStruct((bsz, FOX_HEADS, V_ROWS, seq), BF16)
    in_specs = [pl.BlockSpec((None, tm, d), lambda b, t: (b, t, 0)),
                _const_spec(w.shape), _const_spec(wf.shape), _const_spec(bf.shape),
                pl.BlockSpec((None, 1, LANES), lambda b, t: (b, 0, 0))]
    args = [h16, w, wf, bf, d0]
    aliases = {len(args) + i: i for i in range(len(stacked))}
    in_specs += [pl.BlockSpec(memory_space=pl.ANY)] * len(stacked)
    args += list(stacked)
    return pl.pallas_call(
        kern,
        grid=(bsz, seq // tm),
        in_specs=in_specs,
        out_specs=[heads, heads, pl.BlockSpec((None, None, tm, FOX_HEADS), lambda b, t: (layer, b, t, 0)),
                   cols, rows, v_cols],
        out_shape=[jax.ShapeDtypeStruct(s.shape, s.dtype) for s in stacked] + [cols_shape, rows_shape, v_cols_shape],
        scratch_shapes=[pltpu.VMEM((1, LANES), F32)],
        input_output_aliases=aliases,
        compiler_params=_params("parallel", "arbitrary"),
        name="fox_proj",
    )(*args)


def _fox_cumlog_kernel(lf_ref, d_ref, dend_ref):
    x = lf_ref[...]
    heads, past = x.shape
    lane = lax.broadcasted_iota(jnp.int32, x.shape, 1)
    shift = 1
    while shift < past:
        x = x + jnp.where(lane >= shift, pltpu.roll(x, shift, 1), 0.0)
        shift *= 2
    d_ref[...] = x
    total = jnp.broadcast_to(x[:, past - 1:past], (heads, LANES))
    sub = lax.broadcasted_iota(jnp.int32, (heads, LANES), 0)
    col = lax.broadcasted_iota(jnp.int32, (heads, LANES), 1)
    dend_ref[...] = jnp.sum(jnp.where(sub == col, total, 0.0), axis=0, keepdims=True)


def _fox_cumlog(lf_t, layer):
    _, bsz, heads, past = lf_t.shape
    return pl.pallas_call(
        _fox_cumlog_kernel,
        grid=(bsz,),
        in_specs=[pl.BlockSpec((None, None, heads, past), lambda b: (layer, b, 0, 0))],
        out_specs=[pl.BlockSpec((None, heads, past), lambda b: (b, 0, 0)),
                   pl.BlockSpec((None, 1, LANES), lambda b: (b, 0, 0))],
        out_shape=[jax.ShapeDtypeStruct((bsz, heads, past), F32), jax.ShapeDtypeStruct((bsz, 1, LANES), F32)],
        compiler_params=_params("parallel"),
        name="fox_cumlog",
    )(lf_t)


def _fox_decode_kernel(q_ref, kt_ref, vt_ref, d_ref, kn_ref, vn_ref, o_ref, *, dh):
    j = pl.program_id(1)
    tq = q_ref.shape[2]
    past = kt_ref.shape[2]
    sub = lax.broadcasted_iota(jnp.int32, (8, past), 0)
    pad = jnp.zeros((LANES - dh - 8, past), F32)
    ones_blk = jnp.where(sub == 0, 1.0, 0.0)
    causal = lax.broadcasted_iota(jnp.int32, (tq, tq), 1) <= lax.broadcasted_iota(jnp.int32, (tq, tq), 0)
    lane = lax.broadcasted_iota(jnp.int32, (tq, LANES), 1)
    outs = []
    for hh in range(2):
        odd = hh == 1
        d = d_ref[pl.ds(2 * j + hh, 1), :]
        hi = d.astype(BF16).astype(F32)
        r1 = d - hi
        mid = r1.astype(BF16).astype(F32)
        lo = (r1 - mid).astype(BF16).astype(F32)
        bias = jnp.where(sub == 3, -hi, jnp.where(sub == 4, -mid, jnp.where(sub == 5, -lo,
                         jnp.where(sub < 3, 1.0, 0.0))))
        kt, vt = kt_ref[hh], vt_ref[hh]
        k_ext = jnp.concatenate([bias, pad, kt] if odd else [kt, bias, pad], axis=0).astype(BF16)
        v_ext = jnp.concatenate([ones_blk, pad, vt] if odd else [vt, ones_blk, pad], axis=0).astype(BF16)
        qt = q_ref[hh]
        s_c = _dot_tn(qt, k_ext)
        kn_t = _transpose_bf16(kn_ref[hh].astype(F32)).astype(BF16)
        s_n = jnp.where(causal, _dot_tn(qt, kn_t), -jnp.inf)
        m = jnp.maximum(jnp.max(s_c, axis=-1, keepdims=True), jnp.max(s_n, axis=-1, keepdims=True))
        vn = vn_ref[hh].astype(F32)
        fill = jnp.zeros((LANES - dh - 8, tq), F32)
        vn = jnp.concatenate([vn[dh:dh + 8], fill, vn[:dh]] if odd else [vn[:dh + 8], fill], axis=0)
        acc = (_dot_nt(jnp.exp(s_c - m).astype(BF16), v_ext)
               + _dot_nt(jnp.exp(s_n - m).astype(BF16), vn.astype(BF16)))
        outs.append(acc / (acc[:, 0:1] if odd else acc[:, dh:dh + 1]))
    o_ref[...] = jnp.where(lane < dh, outs[0], outs[1]).astype(o_ref.dtype)


def _fox_decode(q_t, kt, vt, dcum, kn, vn_t, layer, *, dh):
    bsz, heads, _, seq = q_t.shape
    past = kt.shape[-1]
    new_t = pl.BlockSpec((None, 2, LANES, seq), lambda b, j: (b, j, 0, 0))
    cache = pl.BlockSpec((None, None, 2, dh, past), lambda b, j: (layer, b, j, 0, 0))
    return pl.pallas_call(
        functools.partial(_fox_decode_kernel, dh=dh),
        grid=(bsz, heads // 2),
        in_specs=[new_t, cache, cache, pl.BlockSpec((None, heads, past), lambda b, j: (b, 0, 0)),
                  pl.BlockSpec((None, 2, seq, LANES), lambda b, j: (b, j, 0, 0)),
                  pl.BlockSpec((None, 2, V_ROWS, seq), lambda b, j: (b, j, 0, 0))],
        out_specs=pl.BlockSpec((None, seq, LANES), lambda b, j: (b, 0, j)),
        out_shape=jax.ShapeDtypeStruct((bsz, seq, heads // 2 * LANES), BF16),
        compiler_params=_params("parallel", "parallel"),
        name="fox_decode",
    )(q_t, kt, vt, dcum, kn, vn_t)


def _fox_attn_kernel(q_ref, kp_ref, vp_ref, kn_ref, vn_ref, o_ref, s_scr, sd_scr, acc_scr,
                     *, tk, dh, full_base, full_per_q):
    tq = q_ref.shape[2]
    n_full = full_base + pl.program_id(2) * full_per_q
    causal = lax.broadcasted_iota(jnp.int32, (tq, tq), 0) <= lax.broadcasted_iota(jnp.int32, (tq, tq), 1)
    runs, first, left = [], 0, n_full
    for group in ATTN_GROUPS:
        count = left // group
        runs.append((first, group * tk, count))
        first = first + count * group * tk
        left = left - count * group
    halves = []
    for hh in range(2):
        q = q_ref[hh]

        def scores(first, width, q=q, hh=hh):
            def step(i, m):
                rows = pl.ds(pl.multiple_of(first + i * width, tk), width)
                s = _dot(kp_ref[hh, rows, :], q)
                s_scr[rows, :] = s
                return jnp.maximum(m, jnp.max(s, axis=0, keepdims=True))
            return step

        m = jnp.full((1, tq), -jnp.inf, F32)
        for first, width, count in runs:
            m = lax.fori_loop(0, count, scores(first, width), m)
        s = jnp.where(causal, _dot(kn_ref[hh], q), -jnp.inf)
        sd_scr[...] = s
        m = jnp.maximum(m, jnp.max(s, axis=0, keepdims=True))
        acc_scr[hh] = _dot(vn_ref[hh], jnp.exp(sd_scr[...] - m).astype(BF16))

        def accumulate(first, width, m=m, hh=hh):
            def step(i, carry):
                rows = pl.ds(pl.multiple_of(first + i * width, tk), width)
                p = jnp.exp(s_scr[rows, :] - m).astype(BF16)
                acc_scr[hh] += _dot(vp_ref[hh, :, rows], p)
                return carry
            return step

        for first, width, count in runs:
            lax.fori_loop(0, count, accumulate(first, width), 0)
        acc = acc_scr[hh]
        halves.append(acc[:dh] / acc[dh:dh + 1])
    o_ref[...] = _transpose_bf16(jnp.concatenate(halves, axis=0)).astype(o_ref.dtype)


def _fox_attn(q_t, kp, vp_t, kn, vn_t, *, dh):
    bsz, heads, _, seq = q_t.shape
    self_attn = kp is kn
    past = kp.shape[2]
    off = 0 if self_attn else past
    tq = _row_tile(seq, 512)
    nq = seq // tq
    tk = min(tq, 2 * LANES) if self_attn else _row_tile(past, 512)
    assert off % tk == 0 and (nq == 1 or tq % tk == 0), (off, tq, tk)
    full_base, full_per_q = off // tk, tq // tk
    max_full = max(full_base + (nq - 1) * full_per_q, 1)
    kern = functools.partial(_fox_attn_kernel, tk=tk, dh=dh, full_base=full_base, full_per_q=full_per_q)
    return pl.pallas_call(
        kern,
        grid=(bsz, heads // 2, nq),
        in_specs=[pl.BlockSpec((None, 2, LANES, tq), lambda b, j, i: (b, j, 0, i)),
                  pl.BlockSpec((None, 2, past, LANES), lambda b, j, i: (b, j, 0, 0)),
                  pl.BlockSpec((None, 2, V_ROWS, past), lambda b, j, i: (b, j, 0, 0)),
                  pl.BlockSpec((None, 2, tq, LANES), lambda b, j, i: (b, j, i, 0)),
                  pl.BlockSpec((None, 2, V_ROWS, tq), lambda b, j, i: (b, j, 0, i))],
        out_specs=pl.BlockSpec((None, tq, LANES), lambda b, j, i: (b, i, j)),
        out_shape=jax.ShapeDtypeStruct((bsz, seq, heads // 2 * LANES), BF16),
        scratch_shapes=[pltpu.VMEM((max_full * tk, tq), F32), pltpu.VMEM((tq, tq), F32),
                        pltpu.VMEM((2, V_ROWS, tq), F32)],
        compiler_params=_params("parallel", "parallel", "arbitrary"),
        name="fox_attn",
    )(q_t, kp, vp_t, kn, vn_t)


def _merge_kernel(h32_ref, h16_ref, og_ref, om_ref, of_ref, wg_ref, wb_ref, wo_ref, lg_ref, lb_ref,
                  o32_ref, o16_ref, *, alpha):
    d = h32_ref.shape[1]
    x = h16_ref[...]
    m = None
    for r, o_ref in enumerate((og_ref, om_ref, of_ref)):
        gate = jax.nn.sigmoid(_dot(x, wg_ref[:, r * d:(r + 1) * d]))
        term = gate * _dot(o_ref[...], wb_ref[r])
        m = term if m is None else m + term
    y = alpha * h32_ref[...] + _dot(m.astype(BF16), wo_ref[...])
    y = _layer_norm(y, lg_ref[...], lb_ref[...])
    o32_ref[...] = y
    o16_ref[...] = y.astype(BF16)


def _merge(h32, h16, o_gla, o_gm, o_fox, wg, wb, wo, lg, lb, alpha):
    n, d = h32.shape
    w = o_gla.shape[1]
    tm = _row_tile(n, 256)
    row = pl.BlockSpec((tm, d), lambda i: (i, 0))
    mix = pl.BlockSpec((tm, w), lambda i: (i, 0))
    return pl.pallas_call(
        functools.partial(_merge_kernel, alpha=alpha),
        grid=(n // tm,),
        in_specs=[row, row, mix, mix, mix, _const_spec(wg.shape), _const_spec(wb.shape),
                  _const_spec(wo.shape), _const_spec(lg.shape), _const_spec(lb.shape)],
        out_specs=[row, row],
        out_shape=[jax.ShapeDtypeStruct((n, d), F32), jax.ShapeDtypeStruct((n, d), BF16)],
        compiler_params=_params("parallel"),
        name="merge",
    )(h32, h16, o_gla, o_gm, o_fox, wg, wb, wo, lg, lb)


def _ffn_kernel(h32_ref, h16_ref, wg_ref, wu_ref, wd_ref, lg_ref, lb_ref, o32_ref, o16_ref, acc_scr, *, alpha):
    j = pl.program_id(1)

    @pl.when(j == 0)
    def _():
        acc_scr[...] = jnp.zeros_like(acc_scr)

    x = h16_ref[...]
    a = _silu(_dot(x, wg_ref[...])) * _dot(x, wu_ref[...])
    acc_scr[...] += _dot(a.astype(BF16), wd_ref[...])

    @pl.when(j == pl.num_programs(1) - 1)
    def _():
        y = _layer_norm(alpha * h32_ref[...] + acc_scr[...], lg_ref[...], lb_ref[...])
        o32_ref[...] = y
        o16_ref[...] = y.astype(BF16)


def _ffn(h32, h16, wg, wu, wd, lg, lb, alpha):
    n, d = h32.shape
    f = wg.shape[1]
    tm = _row_tile(n, 1024)
    tf = _ff_tile(f)
    row = pl.BlockSpec((tm, d), lambda i, j: (i, 0))
    return pl.pallas_call(
        functools.partial(_ffn_kernel, alpha=alpha),
        grid=(n // tm, f // tf),
        in_specs=[row, row, pl.BlockSpec((d, tf), lambda i, j: (0, j)), pl.BlockSpec((d, tf), lambda i, j: (0, j)),
                  pl.BlockSpec((tf, d), lambda i, j: (j, 0)), _const_spec(lg.shape), _const_spec(lb.shape)],
        out_specs=[row, row],
        out_shape=[jax.ShapeDtypeStruct((n, d), F32), jax.ShapeDtypeStruct((n, d), BF16)],
        scratch_shapes=[pltpu.VMEM((tm, d), F32)],
        compiler_params=_params("parallel", "arbitrary"),
        name="ffn",
    )(h32, h16, wg, wu, wd, lg, lb)


R_W1, R_W2, R_E1, R_E2, R_RANK1, R_RANK2 = range(6)


def _route_kernel(h_ref, w_ref, r_ref, cnt_ref, cnt_scr, *, n_experts):
    @pl.when(pl.program_id(0) == 0)
    def _():
        cnt_scr[...] = jnp.zeros_like(cnt_scr)

    tm = h_ref.shape[0]
    x_hi, x_lo, _ = _split3(h_ref[...])
    w_hi, w_lo, _ = _split3(w_ref[...])
    logits = _dot(x_hi, w_hi) + _dot(x_hi, w_lo) + _dot(x_lo, w_hi)
    lane = lax.broadcasted_iota(jnp.int32, logits.shape, 1)
    logits = jnp.where(lane < n_experts, logits, -jnp.inf)
    v1 = jnp.max(logits, axis=-1, keepdims=True)
    i1 = jnp.min(jnp.where(logits == v1, lane, LANES), axis=-1, keepdims=True)
    rest = jnp.where(lane == i1, -jnp.inf, logits)
    v2 = jnp.max(rest, axis=-1, keepdims=True)
    i2 = jnp.min(jnp.where(rest == v2, lane, LANES), axis=-1, keepdims=True)
    e2 = jnp.exp(v2 - v1)
    w1 = 1.0 / (1.0 + e2)
    w2 = e2 / (1.0 + e2)
    hot1 = lane == i1
    hot2 = lane == i2
    sel = jnp.where(hot1, 1.0, jnp.where(hot2, 1.0, 0.0))
    tri_bf = jnp.where(_lower_tri(tm), 1.0, 0.0).astype(BF16)
    incl = _dot(tri_bf, sel.astype(BF16)) + cnt_scr[...]
    excl = incl - sel
    rank1 = jnp.sum(jnp.where(hot1, excl, 0.0), axis=-1, keepdims=True)
    rank2 = jnp.sum(jnp.where(hot2, excl, 0.0), axis=-1, keepdims=True)
    cnt_scr[...] = incl[tm - 1:tm, :]
    cnt_ref[...] = incl[tm - 1:tm, :]
    rec = jnp.zeros(logits.shape, F32)
    for slot, val in ((R_W1, w1), (R_W2, w2), (R_E1, i1.astype(F32)), (R_E2, i2.astype(F32)),
                      (R_RANK1, rank1), (R_RANK2, rank2)):
        rec = jnp.where(lane == slot, val, rec)
    r_ref[...] = rec


def _route(h32, w_router):
    n, d = h32.shape
    n_experts = w_router.shape[1]
    w_pad = jnp.pad(w_router, ((0, 0), (0, LANES - n_experts)))
    tm = _row_tile(n, 512)
    return pl.pallas_call(
        functools.partial(_route_kernel, n_experts=n_experts),
        grid=(n // tm,),
        in_specs=[pl.BlockSpec((tm, d), lambda i: (i, 0)), _const_spec(w_pad.shape)],
        out_specs=[pl.BlockSpec((tm, LANES), lambda i: (i, 0)), _const_spec((1, LANES))],
        out_shape=[jax.ShapeDtypeStruct((n, LANES), F32), jax.ShapeDtypeStruct((1, LANES), F32)],
        scratch_shapes=[pltpu.VMEM((1, LANES), F32)],
        compiler_params=_params("arbitrary"),
        name="route",
    )(h32, w_pad)


def _for_each_row(n_rows, fn):
    assert n_rows % DMA_UNROLL == 0, n_rows

    def trip(i, carry):
        for j in range(DMA_UNROLL):
            fn(i * DMA_UNROLL + j)
        return carry

    lax.fori_loop(0, n_rows // DMA_UNROLL, trip, 0)


def _dispatch_kernel(pos_ref, h_ref, xin_ref, xg_ref, sem):
    del xin_ref
    tm = pos_ref.shape[1]

    def row_copy(r, s):
        return pltpu.make_async_copy(h_ref.at[pl.ds(r, 1)], xg_ref.at[pl.ds(pos_ref[s, r], 1)], sem)

    _for_each_row(tm, lambda r: [row_copy(r, s).start() for s in range(TOP_K)])
    _for_each_row(tm, lambda r: [row_copy(r, s).wait() for s in range(TOP_K)])


def _dispatch(pos, h32, n_rows):
    n, d = h32.shape
    tm = _row_tile(n, 1024)
    return pl.pallas_call(
        _dispatch_kernel,
        grid=(n // tm,),
        in_specs=[pl.BlockSpec((TOP_K, tm), lambda i: (0, i), memory_space=pltpu.SMEM),
                  pl.BlockSpec((tm, d), lambda i: (i, 0)), pl.BlockSpec(memory_space=pl.ANY)],
        out_specs=pl.BlockSpec(memory_space=pl.ANY),
        out_shape=jax.ShapeDtypeStruct((n_rows, d), F32),
        scratch_shapes=[pltpu.SemaphoreType.DMA(())],
        input_output_aliases={2: 0},
        compiler_params=_params("arbitrary"),
        name="moe_dispatch",
    )(pos, h32, jnp.zeros((n_rows, d), F32))


def _group_ffn_kernel(te_ref, nt_ref, x_ref, wg_ref, wu_ref, wd_ref, y_ref, acc_scr, x16_scr):
    del te_ref
    j = pl.program_id(1)
    last = pl.num_programs(1) - 1
    live = pl.program_id(0) < nt_ref[0]

    @pl.when(live & (j == 0))
    def _():
        x16_scr[...] = x_ref[...].astype(BF16)
        acc_scr[...] = jnp.zeros_like(acc_scr)

    @pl.when(live)
    def _():
        x = x16_scr[...]
        a = _silu(_dot(x, wg_ref[...])) * _dot(x, wu_ref[...])
        acc_scr[...] += _dot(a.astype(BF16), wd_ref[...])

    @pl.when(live & (j == last))
    def _():
        y_ref[...] = acc_scr[...]

    @pl.when(jnp.logical_not(live) & (j == last))
    def _():
        y_ref[...] = jnp.zeros_like(y_ref)


def _group_ffn(tile_expert, n_tiles, xg, wg, wu, wd, tm):
    n_rows, d = xg.shape
    f = wg.shape[2]
    tf = _ff_tile(f)
    nf = f // tf

    def col(i, j, te, nt):
        return jnp.where(i < nt[0], j, nf - 1)

    grid_spec = pltpu.PrefetchScalarGridSpec(
        num_scalar_prefetch=2,
        grid=(n_rows // tm, nf),
        in_specs=[pl.BlockSpec((tm, d), lambda i, j, te, nt: (i, 0)),
                  pl.BlockSpec((None, d, tf), lambda i, j, te, nt: (te[i], 0, col(i, j, te, nt))),
                  pl.BlockSpec((None, d, tf), lambda i, j, te, nt: (te[i], 0, col(i, j, te, nt))),
                  pl.BlockSpec((None, tf, d), lambda i, j, te, nt: (te[i], col(i, j, te, nt), 0))],
        out_specs=pl.BlockSpec((tm, d), lambda i, j, te, nt: (i, 0)),
        scratch_shapes=[pltpu.VMEM((tm, d), F32), pltpu.VMEM((tm, d), BF16)],
    )
    return pl.pallas_call(
        _group_ffn_kernel,
        grid_spec=grid_spec,
        out_shape=jax.ShapeDtypeStruct((n_rows, d), F32),
        compiler_params=_params("arbitrary", "arbitrary"),
        name="moe_ffn",
    )(tile_expert, n_tiles, xg, wg, wu, wd)


def _combine_kernel(pos_ref, r_ref, h32_ref, y_ref, lg_ref, lb_ref, o32_ref, o16_ref, g_scr, sem, *, alpha):
    tm = h32_ref.shape[0]

    def row_copy(r, s):
        return pltpu.make_async_copy(y_ref.at[pl.ds(pos_ref[s, r], 1)], g_scr.at[s, pl.ds(r, 1)], sem)

    _for_each_row(tm, lambda r: [row_copy(r, s).start() for s in range(TOP_K)])
    _for_each_row(tm, lambda r: [row_copy(r, s).wait() for s in range(TOP_K)])
    rec = r_ref[...]
    f = rec[:, R_W1:R_W1 + 1] * g_scr[0] + rec[:, R_W2:R_W2 + 1] * g_scr[1]
    y = _layer_norm(alpha * h32_ref[...] + f, lg_ref[...], lb_ref[...])
    o32_ref[...] = y
    o16_ref[...] = y.astype(BF16)


def _combine(pos, rec, h32, yg, lg, lb, alpha):
    n, d = h32.shape
    tm = _row_tile(n, 256)
    row = pl.BlockSpec((tm, d), lambda i: (i, 0))
    return pl.pallas_call(
        functools.partial(_combine_kernel, alpha=alpha),
        grid=(n // tm,),
        in_specs=[pl.BlockSpec((TOP_K, tm), lambda i: (0, i), memory_space=pltpu.SMEM),
                  pl.BlockSpec((tm, LANES), lambda i: (i, 0)), row,
                  pl.BlockSpec(memory_space=pl.ANY), _const_spec(lg.shape), _const_spec(lb.shape)],
        out_specs=[row, row],
        out_shape=[jax.ShapeDtypeStruct((n, d), F32), jax.ShapeDtypeStruct((n, d), BF16)],
        scratch_shapes=[pltpu.VMEM((TOP_K, tm, d), F32), pltpu.SemaphoreType.DMA(())],
        compiler_params=_params("arbitrary"),
        name="moe_combine",
    )(pos, rec, h32, yg, lg, lb)


def _moe(h32, w_router, wg, wu, wd, lg, lb, alpha):
    n, d = h32.shape
    n_experts = wg.shape[0]
    tm = 1024 if n >= 8192 else 256
    max_tiles = (TOP_K * n) // tm + n_experts
    rec, counts = _route(h32, w_router)

    cnt = counts[0, :n_experts].astype(jnp.int32)
    ends = jnp.cumsum((cnt + tm - 1) // tm * tm)
    starts = ends - (cnt + tm - 1) // tm * tm
    expert = rec[:, R_E1:R_E2 + 1].astype(jnp.int32)
    rank = rec[:, R_RANK1:R_RANK2 + 1].astype(jnp.int32)
    pos = (starts[expert] + rank).T
    n_tiles = (ends[-1] // tm).astype(jnp.int32)
    tile_start = jnp.arange(max_tiles, dtype=jnp.int32) * tm
    tile_expert = jnp.minimum(jnp.searchsorted(ends, tile_start, side='right'), n_experts - 1).astype(jnp.int32)
    tile_expert = jnp.where(jnp.arange(max_tiles) < n_tiles, tile_expert, tile_expert[jnp.maximum(n_tiles - 1, 0)])

    xg = _dispatch(pos, h32, max_tiles * tm)
    yg = _group_ffn(tile_expert, n_tiles.reshape(1), xg, wg, wu, wd, tm)
    return _combine(pos, rec, h32, yg, lg, lb, alpha)


def _layer_weights(w, l, d, mix_w):
    rank = w['w_a2'].shape[1]
    qk_w = w['w_a2'].shape[2]
    sizes = (qk_w, qk_w, mix_w, mix_w, rank, mix_w, mix_w, mix_w, FOX_HEADS, mix_w, mix_w, 3 * d)
    offs = [0]
    for s in sizes:
        offs.append(offs[-1] + s)
    w_in = w['w_in'][l]
    col = lambda a, b: w_in[:, offs[a]:offs[b]]
    pad_cols = lambda x: jnp.pad(x, ((0, 0), (0, LANES - x.shape[1])))
    return dict(
        gla_wz=col(0, 4).astype(BF16),
        gla_wa1=pad_cols(col(4, 5)).astype(BF16),
        gla_wa2=jnp.pad(w['w_a2'][l], ((0, LANES - rank), (0, 0))).astype(BF16),
        gla_ba=w['b_a'][l].reshape(1, -1),
        gla_gn=w['gla_norm_g'][l].reshape(1, -1),
        fox_w=col(5, 8).astype(BF16),
        fox_wf=pad_cols(col(8, 9)).astype(BF16),
        fox_bf=pad_cols(w['b_f'][l].reshape(1, -1)),
        gm_w=col(9, 11).astype(BF16),
        gm_g=w['gmlp_norm_g'][l].reshape(1, -1),
        gm_b=w['gmlp_norm_b'][l].reshape(1, -1),
        gate_w=col(11, 12).astype(BF16),
        branch_w=w['w_branch'][l].astype(BF16),
        out_w=w['w_out'][l].astype(BF16),
    )


def _trunk(x, w, gla_s, fox_k, fox_v, fox_lf, keep_gmlp_rows):
    bsz, seq, d = x.shape
    depth = w['w_in'].shape[0]
    mix_w = w['gla_norm_g'].shape[1]
    alpha = (2 * depth) ** 0.25
    n = bsz * seq
    dk = w['w_a2'].shape[2] // GLA_HEADS
    dv = mix_w // GLA_HEADS
    dh = mix_w // FOX_HEADS
    gm_chunk = GMLP_CHUNK if seq % GMLP_CHUNK == 0 else seq

    h32, h16 = _ln_in(x.reshape(n, d), w['ln_in_g'], w['ln_in_b'])
    if fox_k is not None:
        cache_kt = jnp.transpose(fox_k, (0, 1, 3, 4, 2))
        cache_vt = jnp.transpose(fox_v, (0, 1, 3, 4, 2))
        cache_lf_t = jnp.transpose(fox_lf, (0, 1, 3, 2))
    s_out, gm_out = [], []
    fox_out = [jnp.zeros((depth, bsz, seq, FOX_HEADS, dh), F32), jnp.zeros((depth, bsz, seq, FOX_HEADS, dh), F32),
               jnp.zeros((depth, bsz, seq, FOX_HEADS), F32)]
    for l in range(depth):
        lw = _layer_weights(w, l, d, mix_w)
        h16_seq = h16.reshape(bsz, seq, d)

        if gla_s is None:
            s0_t = jnp.zeros((bsz, GLA_HEADS, dv, dk), F32)
        else:
            s0_t = jnp.swapaxes(gla_s[l], -1, -2)
        o_gla, s_t = _gla(h16_seq, lw['gla_wz'], lw['gla_wa1'], lw['gla_wa2'], lw['gla_ba'], lw['gla_gn'], s0_t)
        s_out.append(jnp.swapaxes(s_t, -1, -2))

        gm = _gmlp(h16_seq, lw['gm_w'], lw['gm_g'], lw['gm_b'], w['gmlp_ws'][l][:, :gm_chunk, :gm_chunk],
                   w['gmlp_bs'][l][:, :gm_chunk].T, keep_gmlp_rows)
        o_gm = gm[0]
        if keep_gmlp_rows:
            gm_out.append(gm[1])

        if fox_k is None:
            d0 = jnp.zeros((bsz, 1, LANES), F32)
        else:
            dcum, d0 = _fox_cumlog(cache_lf_t, l)
        *fox_out, q_t, kn, vn_t = _fox_proj(h16_seq, lw['fox_w'], lw['fox_wf'], lw['fox_bf'], d0, l, fox_out)
        if fox_k is None:
            o_fox = _fox_attn(q_t, kn, vn_t, kn, vn_t, dh=dh)
        else:
            o_fox = _fox_decode(q_t, cache_kt, cache_vt, dcum, kn, vn_t, l, dh=dh)

        h32, h16 = _merge(h32, h16, o_gla.reshape(n, mix_w), o_gm.reshape(n, mix_w), o_fox.reshape(n, mix_w),
                          lw['gate_w'], lw['branch_w'], lw['out_w'],
                          w['ln_g'][l, 0].reshape(1, d), w['ln_b'][l, 0].reshape(1, d), alpha)
        lg, lb = w['ln_g'][l, 1].reshape(1, d), w['ln_b'][l, 1].reshape(1, d)
        j = l // 2
        if l % 2 == 0:
            h32, h16 = _ffn(h32, h16, w['ffn_w_gate'][j].astype(BF16), w['ffn_w_up'][j].astype(BF16),
                            w['ffn_w_down'][j].astype(BF16), lg, lb, alpha)
        else:
            h32, h16 = _moe(h32, w['moe_router'][j], w['moe_w_gate'][j].astype(BF16),
                            w['moe_w_up'][j].astype(BF16), w['moe_w_down'][j].astype(BF16), lg, lb, alpha)
    gm_stack = jnp.stack(gm_out) if keep_gmlp_rows else None
    return (h32.reshape(bsz, seq, d), jnp.stack(s_out), *fox_out, gm_stack)


def kernel(x_prompt, x_sample, state_gla, cache_fox_k, cache_fox_v, cache_fox_logf, ln_in_g, ln_in_b, w_in, w_a2, b_a, gla_norm_g, b_f, gmlp_norm_g, gmlp_norm_b, gmlp_ws, gmlp_bs, w_branch, w_out, ln_g, ln_b, ffn_w_gate, ffn_w_up, ffn_w_down, moe_router, moe_w_gate, moe_w_up, moe_w_down):
    w = dict(ln_in_g=ln_in_g, ln_in_b=ln_in_b, w_in=w_in, w_a2=w_a2, b_a=b_a, gla_norm_g=gla_norm_g,
             b_f=b_f, gmlp_norm_g=gmlp_norm_g, gmlp_norm_b=gmlp_norm_b, gmlp_ws=gmlp_ws, gmlp_bs=gmlp_bs,
             w_branch=w_branch, w_out=w_out, ln_g=ln_g, ln_b=ln_b, ffn_w_gate=ffn_w_gate,
             ffn_w_up=ffn_w_up, ffn_w_down=ffn_w_down, moe_router=moe_router, moe_w_gate=moe_w_gate,
             moe_w_up=moe_w_up, moe_w_down=moe_w_down)
    y_p, gla_p, fk_p, fv_p, flf_p, _ = _trunk(x_prompt, w, None, None, None, None, False)
    y_s, gla_s, fk_s, fv_s, flf_s, gmv_s = _trunk(x_sample, w, state_gla, cache_fox_k, cache_fox_v,
                                                  cache_fox_logf, True)
    return (y_p, y_s, gla_p, fk_p, fv_p, flf_p, gla_s, fk_s, fv_s, flf_s, gmv_s)
```
